```python
import math
import jax, jax.numpy as jnp
from jax import lax
import numpy as np

D_MODEL = 2048
BATCH = 1
SEQ = 8192
DEPTH = 1

DIFF_HEADS = 8
DIFF_HEAD_DIM = 64
DIFF_V_DIM = 2 * DIFF_HEAD_DIM
DIFF_WIDTH = DIFF_HEADS * DIFF_V_DIM
SWA_Q_HEADS = 16
SWA_KV_HEADS = 4
SWA_GROUP = SWA_Q_HEADS // SWA_KV_HEADS
SWA_HEAD_DIM = 64
SWA_WIDTH = SWA_Q_HEADS * SWA_HEAD_DIM
SWA_KV_WIDTH = SWA_KV_HEADS * SWA_HEAD_DIM
WINDOW = 128
Q_BLOCK = 128
D_FF = 5504
NORM_EPS = 1e-6
N_BRANCHES = 2
NEG_INF = -1e30
COL_SIZES = (DIFF_WIDTH, DIFF_WIDTH, DIFF_WIDTH, SWA_WIDTH, SWA_KV_WIDTH, SWA_KV_WIDTH, N_BRANCHES * D_MODEL)
IN_COLS = DIFF_WIDTH * 3 + SWA_WIDTH + 2 * SWA_KV_WIDTH + N_BRANCHES * D_MODEL

kernel_name = "hybrid_diffattn_swa_sink_gated_macaron"


def rmsnorm(x, g):
    xf = x.astype(jnp.float32)
    y = xf * lax.rsqrt(jnp.mean(xf * xf, axis=-1, keepdims=True) + NORM_EPS)
    return (y * g.astype(jnp.float32)).astype(x.dtype)


def alibi_slopes(n):
    return jnp.asarray(2.0 ** (-8.0 * np.arange(1, n + 1) / n), dtype=jnp.float32)


def swiglu(h, w_in, w_out):
    gu = h @ w_in
    g, u = jnp.split(gu, 2, axis=-1)
    return (jax.nn.silu(g) * u) @ w_out


def diff_attention(q, k, v, lam_params, subln, lam_init):
    B, S = q.shape[0], q.shape[1]
    nb = S // Q_BLOCK
    scale = DIFF_HEAD_DIM ** -0.5
    q = q.reshape(B, S, DIFF_HEADS, 2, DIFF_HEAD_DIM) * scale
    k = k.reshape(B, S, DIFF_HEADS, 2, DIFF_HEAD_DIM)
    v = v.reshape(B, S, DIFF_HEADS, DIFF_V_DIM)
    lp = lam_params.astype(jnp.float32)
    lam = jnp.exp(jnp.sum(lp[0] * lp[1])) - jnp.exp(jnp.sum(lp[2] * lp[3])) + lam_init
    slopes = alibi_slopes(DIFF_HEADS)
    k_pos = jnp.arange(S, dtype=jnp.int32)
    qb = jnp.moveaxis(q.reshape(B, nb, Q_BLOCK, DIFF_HEADS, 2, DIFF_HEAD_DIM), 1, 0)

    def body(args):
        qi, i = args
        s = jnp.einsum('bqhcd,bkhcd->bhcqk', qi, k).astype(jnp.float32)
        q_pos = i * Q_BLOCK + jnp.arange(Q_BLOCK, dtype=jnp.int32)
        dist_i = q_pos[:, None] - k_pos[None, :]
        dist = dist_i.astype(jnp.float32)
        bias = -slopes[:, None, None, None] * dist
        s = jnp.where(dist_i >= 0, s + bias, NEG_INF)
        p = jax.nn.softmax(s, axis=-1)
        a = p[:, :, 0] - lam * p[:, :, 1]
        return jnp.einsum('bhqk,bkhe->bqhe', a.astype(v.dtype), v)

    o = lax.map(body, (qb, jnp.arange(nb, dtype=jnp.int32)))
    o = jnp.moveaxis(o, 0, 1).reshape(B, S, DIFF_HEADS, DIFF_V_DIM)
    o = rmsnorm(o, subln) * (1.0 - lam_init)
    return o.reshape(B, S, DIFF_WIDTH)


def swa_attention(q, k, v, sinks):
    B, S = q.shape[0], q.shape[1]
    nb = S // Q_BLOCK
    scale = SWA_HEAD_DIM ** -0.5
    qb = (q * scale).reshape(B, nb, Q_BLOCK, SWA_KV_HEADS, SWA_GROUP, SWA_HEAD_DIM)
    kb = k.reshape(B, nb, Q_BLOCK, SWA_KV_HEADS, SWA_HEAD_DIM)
    vb = v.reshape(B, nb, Q_BLOCK, SWA_KV_HEADS, SWA_HEAD_DIM)
    pad = ((0, 0), (1, 0), (0, 0), (0, 0), (0, 0))
    kk = jnp.concatenate([jnp.pad(kb, pad)[:, :-1], kb], axis=2)
    vv = jnp.concatenate([jnp.pad(vb, pad)[:, :-1], vb], axis=2)
    s = jnp.einsum('bnqkgd,bnskd->bnkgqs', qb, kk).astype(jnp.float32)
    blk = jnp.arange(nb, dtype=jnp.int32)[:, None]
    q_pos = blk * Q_BLOCK + jnp.arange(Q_BLOCK, dtype=jnp.int32)[None, :]
    k_pos = blk * Q_BLOCK - Q_BLOCK + jnp.arange(2 * Q_BLOCK, dtype=jnp.int32)[None, :]
    dist_i = q_pos[:, :, None] - k_pos[:, None, :]
    mask = (dist_i >= 0) & (dist_i < WINDOW) & (k_pos[:, None, :] >= 0)
    slopes = alibi_slopes(SWA_Q_HEADS).reshape(SWA_KV_HEADS, SWA_GROUP)
    bias = -slopes[None, :, :, None, None] * dist_i.astype(jnp.float32)[:, None, None]
    s = jnp.where(mask[None, :, None, None], s + bias[None], NEG_INF)
    sink = jnp.broadcast_to(sinks.astype(jnp.float32).reshape(1, 1, SWA_KV_HEADS, SWA_GROUP, 1, 1),
                            s.shape[:-1] + (1,))
    p = jax.nn.softmax(jnp.concatenate([s, sink], axis=-1), axis=-1)[..., :-1]
    o = jnp.einsum('bnkgqs,bnskd->bnqkgd', p.astype(vv.dtype), vv)
    return o.reshape(B, S, SWA_WIDTH)


def setup_inputs(seed: int = 0) -> dict:
    key = jax.random.key(seed)
    ks = jax.random.split(key, 20)
    f32 = jnp.float32

    def w(k, shape, fan_in):
        return jax.random.normal(k, shape, f32) * (fan_in ** -0.5)

    def gain(k, dim):
        return 1.0 + 0.02 * jax.random.normal(k, (DEPTH, dim), f32)

    return {
        "x": jax.random.normal(ks[0], (BATCH, SEQ, D_MODEL), f32),
        "ffn1_norm_pre": gain(ks[1], D_MODEL),
        "ffn1_w_in": w(ks[2], (DEPTH, D_MODEL, 2 * D_FF), D_MODEL),
        "ffn1_w_out": w(ks[3], (DEPTH, D_FF, D_MODEL), D_FF),
        "ffn1_norm_post": gain(ks[4], D_MODEL),
        "mix_norm_pre": gain(ks[5], D_MODEL),
        "w_in": w(ks[6], (DEPTH, D_MODEL, IN_COLS), D_MODEL),
        "diff_lambda": 0.1 * jax.random.normal(ks[7], (DEPTH, 4, DIFF_HEAD_DIM), f32),
        "diff_subln": gain(ks[8], DIFF_V_DIM),
        "swa_sinks": 0.5 * jax.random.normal(ks[9], (DEPTH, SWA_Q_HEADS), f32),
        "w_branch_diff": w(ks[10], (DEPTH, DIFF_WIDTH, D_MODEL), DIFF_WIDTH),
        "w_branch_swa": w(ks[11], (DEPTH, SWA_WIDTH, D_MODEL), SWA_WIDTH),
        "w_out": w(ks[12], (DEPTH, D_MODEL, D_MODEL), D_MODEL),
        "mix_norm_post": gain(ks[13], D_MODEL),
        "ffn2_norm_pre": gain(ks[14], D_MODEL),
        "ffn2_w_in": w(ks[15], (DEPTH, D_MODEL, 2 * D_FF), D_MODEL),
        "ffn2_w_out": w(ks[16], (DEPTH, D_FF, D_MODEL), D_FF),
        "ffn2_norm_post": gain(ks[17], D_MODEL),
    }


def reference(x, ffn1_norm_pre, ffn1_w_in, ffn1_w_out, ffn1_norm_post, mix_norm_pre, w_in,
              diff_lambda, diff_subln, swa_sinks, w_branch_diff, w_branch_swa, w_out,
              mix_norm_post, ffn2_norm_pre, ffn2_w_in, ffn2_w_out, ffn2_norm_post):
    split_idx = []
    acc = 0
    for c in COL_SIZES[:-1]:
        acc += c
        split_idx.append(acc)
    for l in range(DEPTH):
        lam_init = 0.8 - 0.6 * math.exp(-0.3 * l)
        h = rmsnorm(x, ffn1_norm_pre[l])
        x = x + 0.5 * rmsnorm(swiglu(h, ffn1_w_in[l], ffn1_w_out[l]), ffn1_norm_post[l])
        h = rmsnorm(x, mix_norm_pre[l])
        proj = h @ w_in[l]
        dq, dk, dv, sq, sk, sv, gates = jnp.split(proj, split_idx, axis=-1)
        B, S = x.shape[0], x.shape[1]
        sk = sk.reshape(B, S, SWA_KV_HEADS, SWA_HEAD_DIM)
        sv = sv.reshape(B, S, SWA_KV_HEADS, SWA_HEAD_DIM)
        ya = diff_attention(dq, dk, dv, diff_lambda[l], diff_subln[l], lam_init) @ w_branch_diff[l]
        yb = swa_attention(sq, sk, sv, swa_sinks[l]) @ w_branch_swa[l]
        ga, gb = jnp.split(gates, 2, axis=-1)
        y = (jax.nn.sigmoid(ga) * ya + jax.nn.sigmoid(gb) * yb) @ w_out[l]
        x = x + rmsnorm(y, mix_norm_post[l])
        h = rmsnorm(x, ffn2_norm_pre[l])
        x = x + 0.5 * rmsnorm(swiglu(h, ffn2_w_in[l], ffn2_w_out[l]), ffn2_norm_post[l])
    return x
```

```python
import functools
import math

import numpy as np
import jax
import jax.numpy as jnp
from jax import lax
from jax.experimental import pallas as pl
from jax.experimental.pallas import tpu as pltpu

D_MODEL = 2048
SEQ = 8192
DIFF_HEADS = 8
DIFF_HEAD_DIM = 64
DIFF_V_DIM = 128
DIFF_WIDTH = 1024
SWA_Q_HEADS = 16
SWA_KV_HEADS = 4
SWA_GROUP = 4
SWA_HEAD_DIM = 64
SWA_WIDTH = 1024
SWA_KV_WIDTH = 256
WINDOW = 128
D_FF = 5504
NORM_EPS = 1e-6
NEG_INF = -1e30

LANES = 128
MXU_DIM = 256
VMEM_LIMIT_BYTES = 56 * 1024 * 1024

D_FF_PAD = ((D_FF + 2 * MXU_DIM - 1) // (2 * MXU_DIM)) * (2 * MXU_DIM)
N_GATE = 2 * D_MODEL
PROJ_COLS = N_GATE + DIFF_WIDTH + SWA_KV_WIDTH
PROJT_ROWS = DIFF_WIDTH + DIFF_WIDTH + SWA_WIDTH + SWA_KV_WIDTH
N_SLABS = SEQ // LANES

BF16 = jnp.bfloat16
F32 = jnp.float32


def _dot(a, b):
    return jnp.dot(a, b, preferred_element_type=F32)


def _rms(x):
    return x * lax.rsqrt(jnp.mean(x * x, axis=-1, keepdims=True) + NORM_EPS)


def _params(*sem):
    return pltpu.CompilerParams(dimension_semantics=sem, vmem_limit_bytes=VMEM_LIMIT_BYTES)


def _ffn_body(*refs, emit_next):
    if emit_next:
        (x_ref, gpre_ref, wg_ref, wu_ref, wo_ref, gpost_ref, gnext_ref,
         o_ref, hn_ref, h_scr, acc_scr) = refs
    else:
        (x_ref, gpre_ref, wg_ref, wu_ref, wo_ref, gpost_ref,
         o_ref, h_scr, acc_scr) = refs
    j = pl.program_id(1)

    @pl.when(j == 0)
    def _():
        h_scr[...] = (_rms(x_ref[...]) * gpre_ref[...]).astype(BF16)
        acc_scr[...] = jnp.zeros_like(acc_scr)

    h = h_scr[...]
    g = _dot(h, wg_ref[...])
    u = _dot(h, wu_ref[...])
    act = (jax.nn.silu(g) * u).astype(BF16)
    acc_scr[...] += _dot(act, wo_ref[...])

    @pl.when(j == pl.num_programs(1) - 1)
    def _():
        x1 = x_ref[...] + 0.5 * (_rms(acc_scr[...]) * gpost_ref[...])
        o_ref[...] = x1
        if emit_next:
            hn_ref[...] = (_rms(x1) * gnext_ref[...]).astype(BF16)


def _ffn(x, gpre, wg, wu, wo, gpost, gnext=None, *, tm=512, tf=512):
    emit_next = gnext is not None
    s, d = x.shape
    row = lambda i, j: (i, 0)
    fixed = lambda i, j: (0, 0)
    in_specs = [
        pl.BlockSpec((tm, d), row),
        pl.BlockSpec((1, d), fixed),
        pl.BlockSpec((d, tf), lambda i, j: (0, j)),
        pl.BlockSpec((d, tf), lambda i, j: (0, j)),
        pl.BlockSpec((tf, d), lambda i, j: (j, 0)),
        pl.BlockSpec((1, d), fixed),
    ]
    args = [x, gpre, wg, wu, wo, gpost]
    out_shape = [jax.ShapeDtypeStruct((s, d), F32)]
    out_specs = [pl.BlockSpec((tm, d), row)]
    if emit_next:
        in_specs.append(pl.BlockSpec((1, d), fixed))
        args.append(gnext)
        out_shape.append(jax.ShapeDtypeStruct((s, d), BF16))
        out_specs.append(pl.BlockSpec((tm, d), row))
    res = pl.pallas_call(
        functools.partial(_ffn_body, emit_next=emit_next),
        grid=(s // tm, wg.shape[1] // tf),
        in_specs=in_specs,
        out_specs=out_specs,
        out_shape=out_shape,
        scratch_shapes=[pltpu.VMEM((tm, d), BF16), pltpu.VMEM((tm, d), F32)],
        compiler_params=_params("arbitrary", "arbitrary"),
        name="ffn_next" if emit_next else "ffn",
    )(*args)
    return res if emit_next else res[0]


def _proj_body(h_ref, w_ref, o_ref, *, tn):
    res = _dot(h_ref[...], w_ref[...])
    col = pl.program_id(0) * tn + lax.broadcasted_iota(jnp.int32, res.shape, 1)
    o_ref[...] = jnp.where(col < N_GATE, jax.nn.sigmoid(res), res).astype(BF16)


def _proj(h, w, *, tm=512, tn=768):
    s, d = h.shape
    n = w.shape[1]
    return pl.pallas_call(
        functools.partial(_proj_body, tn=tn),
        grid=(n // tn, s // tm),
        in_specs=[pl.BlockSpec((tm, d), lambda j, i: (i, 0)),
                  pl.BlockSpec((d, tn), lambda j, i: (0, j))],
        out_specs=pl.BlockSpec((tm, tn), lambda j, i: (i, j)),
        out_shape=jax.ShapeDtypeStruct((s, n), BF16),
        compiler_params=_params("arbitrary", "arbitrary"),
        name="proj",
    )(h, w)


def _proj_t_body(w_ref, h_ref, o_ref):
    res = lax.dot_general(w_ref[...], h_ref[...], (((1,), (1,)), ((), ())),
                          preferred_element_type=F32)
    for sl in range(o_ref.shape[0]):
        o_ref[sl] = res[:, sl * LANES:(sl + 1) * LANES].astype(BF16)


def _proj_t(wt, h, *, tm=512, tn=1664):
    s, d = h.shape
    n = wt.shape[0]
    return pl.pallas_call(
        _proj_t_body,
        grid=(n // tn, s // tm),
        in_specs=[pl.BlockSpec((tn, d), lambda j, i: (j, 0)),
                  pl.BlockSpec((tm, d), lambda j, i: (i, 0))],
        out_specs=pl.BlockSpec((tm // LANES, tn, LANES), lambda j, i: (i, j, 0)),
        out_shape=jax.ShapeDtypeStruct((s // LANES, n, LANES), BF16),
        compiler_params=_params("arbitrary", "arbitrary"),
        name="proj_t",
    )(wt, h)


DIFF_TQ = 512
DIFF_TK = 256


def _diff_body(slopes_ref, lam_ref, q_ref, k_ref, v_ref, g_ref, o_ref,
               kcat_scr, qaug_scr, m_scr, l_scr, acc_scr, *, lam_init):
    tq, tk = DIFF_TQ, DIFF_TK
    h = pl.program_id(0)
    i = pl.program_id(1)
    slope = slopes_ref[h]

    @pl.when((h == 0) & (i == 0))
    def _():
        r = lax.broadcasted_iota(jnp.int32, (tk, LANES), 0)
        c = lax.broadcasted_iota(jnp.int32, (tk, LANES), 1)
        pat = jnp.where(c == 0, r, 0).astype(F32).astype(BF16)
        for t in range(SEQ // tk):
            kcat_scr[t * tk:(t + 1) * tk, LANES:2 * LANES] = pat

    @pl.when(i == 0)
    def _():
        kcat_scr[:, 0:LANES] = k_ref[...]

    q = jnp.concatenate([q_ref[s] for s in range(tq // LANES)], axis=1)
    qs = (q.astype(F32) * (DIFF_HEAD_DIM ** -0.5)).astype(BF16)
    row = lax.broadcasted_iota(jnp.int32, (LANES, tq), 0)
    zero = jnp.zeros_like(qs)
    srow = jnp.where(row == 0, slope, 0.0).astype(BF16)
    qaug_scr[0, 0:LANES, :] = jnp.where(row < DIFF_HEAD_DIM, qs, zero)
    qaug_scr[1, 0:LANES, :] = jnp.where(row >= DIFF_HEAD_DIM, qs, zero)
    qaug_scr[0, LANES:2 * LANES, :] = srow
    qaug_scr[1, LANES:2 * LANES, :] = srow
    m_scr[...] = jnp.full_like(m_scr, NEG_INF)
    l_scr[...] = jnp.zeros_like(l_scr)
    acc_scr[...] = jnp.zeros_like(acc_scr)

    def step(j, diag_offset):
        kc = kcat_scr[pl.ds(pl.multiple_of(j * tk, tk), tk), :]
        vt = jnp.concatenate([v_ref[(tk // LANES) * j + s] for s in range(tk // LANES)], axis=1)
        koff = (jnp.zeros((1, tq), jnp.int32) + j * tk).astype(F32) * slope
        if diag_offset is not None:
            kr = lax.broadcasted_iota(jnp.int32, (tk, tq), 0) + diag_offset
            qc = lax.broadcasted_iota(jnp.int32, (tk, tq), 1)
            keep = kr <= qc
        for c in range(2):
            z = _dot(kc, qaug_scr[c])
            if diag_offset is not None:
                z = jnp.where(keep, z, NEG_INF)
            m_old = m_scr[c]
            m_new = jnp.maximum(m_old, jnp.max(z, axis=0, keepdims=True) + koff)
            alpha = jnp.exp(m_old - m_new)
            p = jnp.exp(z - (m_new - koff))
            l_scr[c] = alpha * l_scr[c] + jnp.sum(p, axis=0, keepdims=True)
            acc_scr[c] = alpha * acc_scr[c] + _dot(vt, p.astype(BF16))
            m_scr[c] = m_new

    n_full = i * (tq // tk)

    def body(j, carry):
        step(j, None)
        return carry

    lax.fori_loop(0, n_full, body, 0)
    for t in range(tq // tk):
        step(n_full + t, t * tk)

    lp = lam_ref[...]
    lam = (jnp.exp(jnp.sum(lp[0:1, :] * lp[1:2, :], axis=1, keepdims=True))
           - jnp.exp(jnp.sum(lp[2:3, :] * lp[3:4, :], axis=1, keepdims=True)) + lam_init)
    o = acc_scr[0] * (1.0 / l_scr[0]) - lam * (acc_scr[1] * (1.0 / l_scr[1]))
    y = o * lax.rsqrt(jnp.mean(o * o, axis=0, keepdims=True) + NORM_EPS)
    o_ref[...] = ((y.T * g_ref[...]) * (1.0 - lam_init)).astype(BF16)


def _diff_attn(slopes, lam_params, subln, proj, proj_t, *, lam_init):
    tq, tk = DIFF_TQ, DIFF_TK
    k_col0 = N_GATE // LANES
    v_row0 = DIFF_WIDTH // LANES
    return pl.pallas_call(
        functools.partial(_diff_body, lam_init=lam_init),
        grid=(DIFF_HEADS, SEQ // tq),
        in_specs=[
            pl.BlockSpec(memory_space=pltpu.SMEM),
            pl.BlockSpec((4, DIFF_HEAD_DIM), lambda h, i: (0, 0)),
            pl.BlockSpec((tq // LANES, LANES, LANES), lambda h, i: (i, h, 0)),
            pl.BlockSpec((SEQ, LANES), lambda h, i: (0, k_col0 + h)),
            pl.BlockSpec((N_SLABS, LANES, LANES), lambda h, i: (0, v_row0 + h, 0)),
            pl.BlockSpec((1, DIFF_V_DIM), lambda h, i: (0, 0)),
        ],
        out_specs=pl.BlockSpec((tq, DIFF_V_DIM), lambda h, i: (i, h)),
        out_shape=jax.ShapeDtypeStruct((SEQ, DIFF_WIDTH), BF16),
        scratch_shapes=[
            pltpu.VMEM((SEQ, 2 * LANES), BF16),
            pltpu.VMEM((2, 2 * LANES, tq), BF16),
            pltpu.VMEM((2, 1, tq), F32),
            pltpu.VMEM((2, 1, tq), F32),
            pltpu.VMEM((2, DIFF_V_DIM, tq), F32),
        ],
        compiler_params=_params("arbitrary", "arbitrary"),
        name="diff_attn",
    )(slopes, lam_params, proj_t, proj, proj_t, subln)


SWA_TQ = 512
SWA_BLOCKS = SWA_TQ // WINDOW


def _swa_body(slopes_ref, sinks_ref, q_ref, k_ref, v_ref, o_ref, qaug_scr):
    i = pl.program_id(0)

    @pl.when(i == 0)
    def _():
        qaug_scr[...] = jnp.zeros_like(qaug_scr)

    def block(t, carry):
        n = i * SWA_BLOCKS + t
        first = n == 0
        off = jnp.where(first, 0, WINDOW)
        slab0 = jnp.where(first, 0, n - 1)
        kwin = k_ref[pl.ds(pl.multiple_of(slab0 * WINDOW, WINDOW), 2 * WINDOW), :]
        vt = jnp.concatenate([v_ref[slab0], v_ref[slab0 + 1]], axis=1)
        qs = q_ref[t]
        kb = lax.broadcasted_iota(jnp.int32, (2 * WINDOW, WINDOW), 0)
        qa = lax.broadcasted_iota(jnp.int32, (2 * WINDOW, WINDOW), 1)
        dist = off + qa - kb
        valid = (dist >= 0) & (dist < WINDOW)
        distf = dist.astype(F32)
        outs = []
        for kh in range(SWA_KV_HEADS):
            r0 = kh * SWA_HEAD_DIM
            for g in range(SWA_GROUP):
                hd = kh * SWA_GROUP + g
                qh = qs[hd * SWA_HEAD_DIM:(hd + 1) * SWA_HEAD_DIM, :].astype(F32) * (SWA_HEAD_DIM ** -0.5)
                qaug_scr[kh, r0:r0 + SWA_HEAD_DIM, g * LANES:(g + 1) * LANES] = qh.astype(BF16)
            z = _dot(kwin, qaug_scr[kh])
            vt_kh = vt[r0:r0 + SWA_HEAD_DIM, :]
            row = []
            for g in range(SWA_GROUP):
                hd = kh * SWA_GROUP + g
                sink = sinks_ref[hd]
                sc = jnp.where(valid, z[:, g * LANES:(g + 1) * LANES] - slopes_ref[hd] * distf, NEG_INF)
                m = jnp.maximum(jnp.max(sc, axis=0, keepdims=True), sink)
                p = jnp.exp(sc - m)
                denom = jnp.sum(p, axis=0, keepdims=True) + jnp.exp(sink - m)
                row.append(_dot(vt_kh, p.astype(BF16)) * (1.0 / denom))
            outs.append(row)
        r = pl.multiple_of(t * WINDOW, WINDOW)
        for pr in range(SWA_KV_HEADS // 2):
            for g in range(SWA_GROUP):
                x = jnp.concatenate([outs[2 * pr][g], outs[2 * pr + 1][g]], axis=0)
                c0 = (pr * SWA_GROUP + g) * LANES
                o_ref[pl.ds(r, WINDOW), c0:c0 + LANES] = x.T.astype(BF16)
        return carry

    lax.fori_loop(0, SWA_BLOCKS, block, 0)


def _swa_attn(slopes, sinks, proj, proj_t):
    q_row_blk = (2 * DIFF_WIDTH) // SWA_WIDTH
    k_col_blk = (N_GATE + DIFF_WIDTH) // SWA_KV_WIDTH
    v_row_blk = (2 * DIFF_WIDTH + SWA_WIDTH) // SWA_KV_WIDTH
    return pl.pallas_call(
        _swa_body,
        grid=(SEQ // SWA_TQ,),
        in_specs=[
            pl.BlockSpec(memory_space=pltpu.SMEM),
            pl.BlockSpec(memory_space=pltpu.SMEM),
            pl.BlockSpec((SWA_BLOCKS, SWA_WIDTH, LANES), lambda i: (i, q_row_blk, 0)),
            pl.BlockSpec((SEQ, SWA_KV_WIDTH), lambda i: (0, k_col_blk)),
            pl.BlockSpec((N_SLABS, SWA_KV_WIDTH, LANES), lambda i: (0, v_row_blk, 0)),
        ],
        out_specs=pl.BlockSpec((SWA_TQ, SWA_WIDTH), lambda i: (i, 0)),
        out_shape=jax.ShapeDtypeStruct((SEQ, SWA_WIDTH), BF16),
        scratch_shapes=[pltpu.VMEM((SWA_KV_HEADS, SWA_KV_WIDTH, SWA_GROUP * LANES), BF16)],
        compiler_params=_params("arbitrary"),
        name="swa_attn",
    )(slopes, sinks, proj_t, proj, proj_t)


def _mix_body(a_ref, b_ref, ga_ref, gb_ref, wa_ref, wb_ref, wo_ref, x_ref, gpost_ref, o_ref, acc_scr):
    j = pl.program_id(1)

    @pl.when(j == 0)
    def _():
        acc_scr[...] = jnp.zeros_like(acc_scr)

    ya = _dot(a_ref[...], wa_ref[...])
    yb = _dot(b_ref[...], wb_ref[...])
    mixed = ga_ref[...].astype(F32) * ya + gb_ref[...].astype(F32) * yb
    acc_scr[...] += _dot(mixed.astype(BF16), wo_ref[...])

    @pl.when(j == pl.num_programs(1) - 1)
    def _():
        o_ref[...] = x_ref[...] + _rms(acc_scr[...]) * gpost_ref[...]


def _mix_out(a, b, proj, wa, wb, wo, x, gpost, *, tm=512, tn=512):
    s, d = x.shape
    nj = d // tn
    return pl.pallas_call(
        _mix_body,
        grid=(s // tm, nj),
        in_specs=[
            pl.BlockSpec((tm, DIFF_WIDTH), lambda i, j: (i, 0)),
            pl.BlockSpec((tm, SWA_WIDTH), lambda i, j: (i, 0)),
            pl.BlockSpec((tm, tn), lambda i, j: (i, j)),
            pl.BlockSpec((tm, tn), lambda i, j: (i, nj + j)),
            pl.BlockSpec((DIFF_WIDTH, tn), lambda i, j: (0, j)),
            pl.BlockSpec((SWA_WIDTH, tn), lambda i, j: (0, j)),
            pl.BlockSpec((tn, d), lambda i, j: (j, 0)),
            pl.BlockSpec((tm, d), lambda i, j: (i, 0)),
            pl.BlockSpec((1, d), lambda i, j: (0, 0)),
        ],
        out_specs=pl.BlockSpec((tm, d), lambda i, j: (i, 0)),
        out_shape=jax.ShapeDtypeStruct((s, d), F32),
        scratch_shapes=[pltpu.VMEM((tm, d), F32)],
        compiler_params=_params("arbitrary", "arbitrary"),
        name="mix_out",
    )(a, b, proj, proj, wa, wb, wo, x, gpost)


def _alibi_slopes(n):
    return jnp.asarray(2.0 ** (-8.0 * np.arange(1, n + 1) / n), dtype=F32)


def _ffn_weights(w_in, w_out):
    pad = D_FF_PAD - D_FF
    wg = jnp.pad(w_in[:, :D_FF], ((0, 0), (0, pad))).astype(BF16)
    wu = jnp.pad(w_in[:, D_FF:], ((0, 0), (0, pad))).astype(BF16)
    wo = jnp.pad(w_out, ((0, pad), (0, 0))).astype(BF16)
    return wg, wu, wo


def kernel(x, ffn1_norm_pre, ffn1_w_in, ffn1_w_out, ffn1_norm_post, mix_norm_pre, w_in, diff_lambda, diff_subln, swa_sinks, w_branch_diff, w_branch_swa, w_out, mix_norm_post, ffn2_norm_pre, ffn2_w_in, ffn2_w_out, ffn2_norm_post):
    depth = ffn1_w_in.shape[0]
    xs = x.reshape(SEQ, D_MODEL)
    for l in range(depth):
        lam_init = 0.8 - 0.6 * math.exp(-0.3 * l)
        w = w_in[l]
        o_dq, o_dk, o_dv = 0, DIFF_WIDTH, 2 * DIFF_WIDTH
        o_sq = 3 * DIFF_WIDTH
        o_sk = o_sq + SWA_WIDTH
        o_sv = o_sk + SWA_KV_WIDTH
        o_g = o_sv + SWA_KV_WIDTH
        wa = jnp.concatenate([w[:, o_g:], w[:, o_dk:o_dv], w[:, o_sk:o_sv]], axis=1).astype(BF16)
        wbt = jnp.concatenate([w[:, o_dq:o_dk], w[:, o_dv:o_sq], w[:, o_sq:o_sk], w[:, o_sv:o_g]],
                              axis=1).T.astype(BF16)
        wbs = (w_branch_swa[l].reshape(2, 2, SWA_GROUP, SWA_HEAD_DIM, D_MODEL)
               .transpose(0, 2, 1, 3, 4).reshape(SWA_WIDTH, D_MODEL).astype(BF16))
        wbd = w_branch_diff[l].astype(BF16)
        wo = w_out[l].astype(BF16)
        row = lambda v: v.reshape(1, -1)

        wg1, wu1, wo1 = _ffn_weights(ffn1_w_in[l], ffn1_w_out[l])
        x1, h2 = _ffn(xs, row(ffn1_norm_pre[l]), wg1, wu1, wo1, row(ffn1_norm_post[l]), row(mix_norm_pre[l]))
        proj = _proj(h2, wa)
        proj_t = _proj_t(wbt, h2)
        a = _diff_attn(_alibi_slopes(DIFF_HEADS), diff_lambda[l], row(diff_subln[l]), proj, proj_t,
                       lam_init=lam_init)
        b = _swa_attn(_alibi_slopes(SWA_Q_HEADS), swa_sinks[l], proj, proj_t)
        x2 = _mix_out(a, b, proj, wbd, wbs, wo, x1, row(mix_norm_post[l]))
        wg2, wu2, wo2 = _ffn_weights(ffn2_w_in[l], ffn2_w_out[l])
        xs = _ffn(x2, row(ffn2_norm_pre[l]), wg2, wu2, wo2, row(ffn2_norm_post[l]))
    return xs.reshape(x.shape)
```

```python
import functools
import math

import numpy as np
import jax
import jax.numpy as jnp
from jax import lax
from jax.experimental import pallas as pl
from jax.experimental.pallas import tpu as pltpu

D_MODEL = 2048
SEQ = 8192
DIFF_HEADS = 8
DIFF_HEAD_DIM = 64
DIFF_V_DIM = 128
DIFF_WIDTH = 1024
SWA_Q_HEADS = 16
SWA_KV_HEADS = 4
SWA_GROUP = 4
SWA_HEAD_DIM = 64
SWA_WIDTH = 1024
SWA_KV_WIDTH = 256
WINDOW = 128
D_FF = 5504
NORM_EPS = 1e-6
NEG_INF = -1e30

LANES = 128
MXU_DIM = 256
VMEM_LIMIT_BYTES = 56 * 1024 * 1024

D_FF_PAD = ((D_FF + 2 * MXU_DIM - 1) // (2 * MXU_DIM)) * (2 * MXU_DIM)
N_GATE = 2 * D_MODEL
PROJ_COLS = N_GATE + DIFF_WIDTH + SWA_KV_WIDTH
PROJT_ROWS = DIFF_WIDTH + DIFF_WIDTH + SWA_WIDTH + SWA_KV_WIDTH
N_SLABS = SEQ // LANES

BF16 = jnp.bfloat16
F32 = jnp.float32


def _dot(a, b):
    return jnp.dot(a, b, preferred_element_type=F32)


def _rms(x):
    return x * lax.rsqrt(jnp.mean(x * x, axis=-1, keepdims=True) + NORM_EPS)


def _params(*sem):
    return pltpu.CompilerParams(dimension_semantics=sem, vmem_limit_bytes=VMEM_LIMIT_BYTES)


def _ffn_body(*refs, emit_next):
    if emit_next:
        (x_ref, gpre_ref, wg_ref, wu_ref, wo_ref, gpost_ref, gnext_ref,
         o_ref, hn_ref, h_scr, acc_scr) = refs
    else:
        (x_ref, gpre_ref, wg_ref, wu_ref, wo_ref, gpost_ref,
         o_ref, h_scr, acc_scr) = refs
    j = pl.program_id(1)

    @pl.when(j == 0)
    def _():
        h_scr[...] = (_rms(x_ref[...]) * gpre_ref[...]).astype(BF16)
        acc_scr[...] = jnp.zeros_like(acc_scr)

    h = h_scr[...]
    g = _dot(h, wg_ref[...])
    u = _dot(h, wu_ref[...])
    act = (jax.nn.silu(g) * u).astype(BF16)
    acc_scr[...] += _dot(act, wo_ref[...])

    @pl.when(j == pl.num_programs(1) - 1)
    def _():
        x1 = x_ref[...] + 0.5 * (_rms(acc_scr[...]) * gpost_ref[...])
        o_ref[...] = x1
        if emit_next:
            hn_ref[...] = (_rms(x1) * gnext_ref[...]).astype(BF16)


def _ffn(x, gpre, wg, wu, wo, gpost, gnext=None, *, tm=512, tf=512):
    emit_next = gnext is not None
    s, d = x.shape
    row = lambda i, j: (i, 0)
    fixed = lambda i, j: (0, 0)
    in_specs = [
        pl.BlockSpec((tm, d), row),
        pl.BlockSpec((1, d), fixed),
        pl.BlockSpec((d, tf), lambda i, j: (0, j)),
        pl.BlockSpec((d, tf), lambda i, j: (0, j)),
        pl.BlockSpec((tf, d), lambda i, j: (j, 0)),
        pl.BlockSpec((1, d), fixed),
    ]
    args = [x, gpre, wg, wu, wo, gpost]
    out_shape = [jax.ShapeDtypeStruct((s, d), F32)]
    out_specs = [pl.BlockSpec((tm, d), row)]
    if emit_next:
        in_specs.append(pl.BlockSpec((1, d), fixed))
        args.append(gnext)
        out_shape.append(jax.ShapeDtypeStruct((s, d), BF16))
        out_specs.append(pl.BlockSpec((tm, d), row))
    res = pl.pallas_call(
        functools.partial(_ffn_body, emit_next=emit_next),
        grid=(s // tm, wg.shape[1] // tf),
        in_specs=in_specs,
        out_specs=out_specs,
        out_shape=out_shape,
        scratch_shapes=[pltpu.VMEM((tm, d), BF16), pltpu.VMEM((tm, d), F32)],
        compiler_params=_params("arbitrary", "arbitrary"),
        name="ffn_next" if emit_next else "ffn",
    )(*args)
    return res if emit_next else res[0]


def _proj_body(h_ref, w_ref, o_ref, *, tn):
    res = _dot(h_ref[...], w_ref[...])
    col = pl.program_id(0) * tn + lax.broadcasted_iota(jnp.int32, res.shape, 1)
    o_ref[...] = jnp.where(col < N_GATE, jax.nn.sigmoid(res), res).astype(BF16)


def _proj(h, w, *, tm=512, tn=768):
    s, d = h.shape
    n = w.shape[1]
    return pl.pallas_call(
        functools.partial(_proj_body, tn=tn),
        grid=(n // tn, s // tm),
        in_specs=[pl.BlockSpec((tm, d), lambda j, i: (i, 0)),
                  pl.BlockSpec((d, tn), lambda j, i: (0, j))],
        out_specs=pl.BlockSpec((tm, tn), lambda j, i: (i, j)),
        out_shape=jax.ShapeDtypeStruct((s, n), BF16),
        compiler_params=_params("arbitrary", "arbitrary"),
        name="proj",
    )(h, w)


def _proj_t_body(w_ref, h_ref, o_ref):
    res = lax.dot_general(w_ref[...], h_ref[...], (((1,), (1,)), ((), ())),
                          preferred_element_type=F32)
    for sl in range(o_ref.shape[0]):
        o_ref[sl] = res[:, sl * LANES:(sl + 1) * LANES].astype(BF16)


def _proj_t(wt, h, *, tm=512, tn=1664):
    s, d = h.shape
    n = wt.shape[0]
    return pl.pallas_call(
        _proj_t_body,
        grid=(n // tn, s // tm),
        in_specs=[pl.BlockSpec((tn, d), lambda j, i: (j, 0)),
                  pl.BlockSpec((tm, d), lambda j, i: (i, 0))],
        out_specs=pl.BlockSpec((tm // LANES, tn, LANES), lambda j, i: (i, j, 0)),
        out_shape=jax.ShapeDtypeStruct((s // LANES, n, LANES), BF16),
        compiler_params=_params("arbitrary", "arbitrary"),
        name="proj_t",
    )(wt, h)


DIFF_TQ = 512
DIFF_TK = 256
DIFF_ACC_ROWS = DIFF_V_DIM + 16
LOG2E = math.log2(math.e)


def _diff_body(slopes_ref, lam_ref, q_ref, k_ref, v_ref, g_ref, o_ref,
               kcat_scr, qaug_scr, z0_scr, z1_scr, m_scr, acc_scr, *, lam_init):
    tq, tk = DIFF_TQ, DIFF_TK
    h = pl.program_id(0)
    i = pl.program_id(1)
    slope = slopes_ref[h]

    @pl.when((h == 0) & (i == 0))
    def _():
        r = lax.broadcasted_iota(jnp.int32, (tk, LANES), 0)
        c = lax.broadcasted_iota(jnp.int32, (tk, LANES), 1)
        pat = jnp.where(c < 2, r, 0).astype(F32).astype(BF16)
        for t in range(SEQ // tk):
            kcat_scr[t * tk:(t + 1) * tk, LANES:2 * LANES] = pat

    @pl.when(i == 0)
    def _():
        kcat_scr[:, 0:LANES] = k_ref[...]

    q = jnp.concatenate([q_ref[s] for s in range(tq // LANES)], axis=1)
    qs = (q.astype(F32) * (DIFF_HEAD_DIM ** -0.5 * LOG2E)).astype(BF16)
    row = lax.broadcasted_iota(jnp.int32, (LANES, tq), 0)
    zero = jnp.zeros_like(qs)
    s_f32 = jnp.where(row < 2, slope, 0.0)
    s_hi = s_f32.astype(BF16)
    s_lo = (s_f32 - s_hi.astype(F32)).astype(BF16)
    srow = jnp.where(row == 0, s_hi, s_lo)
    qaug_scr[0, 0:LANES, :] = jnp.where(row < DIFF_HEAD_DIM, qs, zero)
    qaug_scr[1, 0:LANES, :] = jnp.where(row >= DIFF_HEAD_DIM, qs, zero)
    qaug_scr[0, LANES:2 * LANES, :] = srow
    qaug_scr[1, LANES:2 * LANES, :] = srow
    m_scr[...] = jnp.full_like(m_scr, NEG_INF)
    acc_scr[...] = jnp.zeros_like(acc_scr)

    def qk(j, z_scr):
        kc = kcat_scr[pl.ds(pl.multiple_of(j * tk, tk), tk), :]
        for c in range(2):
            z_scr[c] = _dot(kc, qaug_scr[c])

    def softmax_pv(j, z_scr, diag_offset):
        vt = jnp.concatenate([v_ref[(tk // LANES) * j + s] for s in range(tk // LANES)], axis=1)
        vt = jnp.concatenate([vt, jnp.ones((DIFF_ACC_ROWS - DIFF_V_DIM, tk), BF16)], axis=0)
        koff = (jnp.zeros((1, tq), jnp.int32) + j * tk).astype(F32) * slope
        if diag_offset is not None:
            kr = lax.broadcasted_iota(jnp.int32, (tk, tq), 0) + diag_offset
            qc = lax.broadcasted_iota(jnp.int32, (tk, tq), 1)
            keep = kr <= qc
        for c in range(2):
            z = z_scr[c]
            if diag_offset is not None:
                z = jnp.where(keep, z, NEG_INF)
            m_old = m_scr[c]
            m_new = jnp.maximum(m_old, jnp.max(z, axis=0, keepdims=True) + koff)
            alpha = jnp.exp2(m_old - m_new)
            p = jnp.exp2(z - (m_new - koff)).astype(BF16)
            acc_scr[c] = alpha * acc_scr[c] + _dot(vt, p)
            m_scr[c] = m_new

    qk(0, z0_scr)

    def pair(jj, carry):
        j0 = 2 * jj
        qk(j0 + 1, z1_scr)
        softmax_pv(j0, z0_scr, None)
        qk(j0 + 2, z0_scr)
        softmax_pv(j0 + 1, z1_scr, None)
        return carry

    lax.fori_loop(0, i, pair, 0)
    j0 = 2 * i
    qk(j0 + 1, z1_scr)
    softmax_pv(j0, z0_scr, 0)
    softmax_pv(j0 + 1, z1_scr, tk)

    lp = lam_ref[...]
    lam = (jnp.exp(jnp.sum(lp[0:1, :] * lp[1:2, :], axis=1, keepdims=True))
           - jnp.exp(jnp.sum(lp[2:3, :] * lp[3:4, :], axis=1, keepdims=True)) + lam_init)
    o0 = acc_scr[0, 0:DIFF_V_DIM, :] * (1.0 / acc_scr[0, DIFF_V_DIM:DIFF_V_DIM + 1, :])
    o1 = acc_scr[1, 0:DIFF_V_DIM, :] * (1.0 / acc_scr[1, DIFF_V_DIM:DIFF_V_DIM + 1, :])
    o = o0 - lam * o1
    y = o * lax.rsqrt(jnp.mean(o * o, axis=0, keepdims=True) + NORM_EPS)
    o_ref[...] = ((y.T * g_ref[...]) * (1.0 - lam_init)).astype(BF16)


def _diff_attn(slopes, lam_params, subln, proj, proj_t, *, lam_init):
    tq, tk = DIFF_TQ, DIFF_TK
    assert tq == 2 * tk
    k_col0 = N_GATE // LANES
    v_row0 = DIFF_WIDTH // LANES
    return pl.pallas_call(
        functools.partial(_diff_body, lam_init=lam_init),
        grid=(DIFF_HEADS, SEQ // tq),
        in_specs=[
            pl.BlockSpec(memory_space=pltpu.SMEM),
            pl.BlockSpec((4, DIFF_HEAD_DIM), lambda h, i: (0, 0)),
            pl.BlockSpec((tq // LANES, LANES, LANES), lambda h, i: (i, h, 0)),
            pl.BlockSpec((SEQ, LANES), lambda h, i: (0, k_col0 + h)),
            pl.BlockSpec((N_SLABS, LANES, LANES), lambda h, i: (0, v_row0 + h, 0)),
            pl.BlockSpec((1, DIFF_V_DIM), lambda h, i: (0, 0)),
        ],
        out_specs=pl.BlockSpec((tq, DIFF_V_DIM), lambda h, i: (i, h)),
        out_shape=jax.ShapeDtypeStruct((SEQ, DIFF_WIDTH), BF16),
        scratch_shapes=[
            pltpu.VMEM((SEQ, 2 * LANES), BF16),
            pltpu.VMEM((2, 2 * LANES, tq), BF16),
            pltpu.VMEM((2, tk, tq), F32),
            pltpu.VMEM((2, tk, tq), F32),
            pltpu.VMEM((2, 1, tq), F32),
            pltpu.VMEM((2, DIFF_ACC_ROWS, tq), F32),
        ],
        compiler_params=_params("arbitrary", "arbitrary"),
        name="diff_attn",
    )(slopes, lam_params, proj_t, proj, proj_t, subln)


SWA_TQ = 512
SWA_BLOCKS = SWA_TQ // WINDOW


def _swa_body(slopes_ref, sinks_ref, q_ref, k_ref, v_ref, o_ref, qaug_scr):
    i = pl.program_id(0)

    @pl.when(i == 0)
    def _():
        qaug_scr[...] = jnp.zeros_like(qaug_scr)

    def block(t, carry):
        n = i * SWA_BLOCKS + t
        first = n == 0
        off = jnp.where(first, 0, WINDOW)
        slab0 = jnp.where(first, 0, n - 1)
        kwin = k_ref[pl.ds(pl.multiple_of(slab0 * WINDOW, WINDOW), 2 * WINDOW), :]
        vt = jnp.concatenate([v_ref[slab0], v_ref[slab0 + 1]], axis=1)
        qs = q_ref[t]
        kb = lax.broadcasted_iota(jnp.int32, (2 * WINDOW, WINDOW), 0)
        qa = lax.broadcasted_iota(jnp.int32, (2 * WINDOW, WINDOW), 1)
        dist = off + qa - kb
        valid = (dist >= 0) & (dist < WINDOW)
        distf = dist.astype(F32)
        outs = []
        for kh in range(SWA_KV_HEADS):
            r0 = kh * SWA_HEAD_DIM
            for g in range(SWA_GROUP):
                hd = kh * SWA_GROUP + g
                qh = qs[hd * SWA_HEAD_DIM:(hd + 1) * SWA_HEAD_DIM, :].astype(F32) * (SWA_HEAD_DIM ** -0.5)
                qaug_scr[kh, r0:r0 + SWA_HEAD_DIM, g * LANES:(g + 1) * LANES] = qh.astype(BF16)
            z = _dot(kwin, qaug_scr[kh])
            vt_kh = vt[r0:r0 + SWA_HEAD_DIM, :]
            row = []
            for g in range(SWA_GROUP):
                hd = kh * SWA_GROUP + g
                sink = sinks_ref[hd]
                sc = jnp.where(valid, z[:, g * LANES:(g + 1) * LANES] - slopes_ref[hd] * distf, NEG_INF)
                m = jnp.maximum(jnp.max(sc, axis=0, keepdims=True), sink)
                p = jnp.exp(sc - m)
                denom = jnp.sum(p, axis=0, keepdims=True) + jnp.exp(sink - m)
                row.append(_dot(vt_kh, p.astype(BF16)) * (1.0 / denom))
            outs.append(row)
        r = pl.multiple_of(t * WINDOW, WINDOW)
        for pr in range(SWA_KV_HEADS // 2):
            for g in range(SWA_GROUP):
                x = jnp.concatenate([outs[2 * pr][g], outs[2 * pr + 1][g]], axis=0)
                c0 = (pr * SWA_GROUP + g) * LANES
                o_ref[pl.ds(r, WINDOW), c0:c0 + LANES] = x.T.astype(BF16)
        return carry

    lax.fori_loop(0, SWA_BLOCKS, block, 0)


def _swa_attn(slopes, sinks, proj, proj_t):
    q_row_blk = (2 * DIFF_WIDTH) // SWA_WIDTH
    k_col_blk = (N_GATE + DIFF_WIDTH) // SWA_KV_WIDTH
    v_row_blk = (2 * DIFF_WIDTH + SWA_WIDTH) // SWA_KV_WIDTH
    return pl.pallas_call(
        _swa_body,
        grid=(SEQ // SWA_TQ,),
        in_specs=[
            pl.BlockSpec(memory_space=pltpu.SMEM),
            pl.BlockSpec(memory_space=pltpu.SMEM),
            pl.BlockSpec((SWA_BLOCKS, SWA_WIDTH, LANES), lambda i: (i, q_row_blk, 0)),
            pl.BlockSpec((SEQ, SWA_KV_WIDTH), lambda i: (0, k_col_blk)),
            pl.BlockSpec((N_SLABS, SWA_KV_WIDTH, LANES), lambda i: (0, v_row_blk, 0)),
        ],
        out_specs=pl.BlockSpec((SWA_TQ, SWA_WIDTH), lambda i: (i, 0)),
        out_shape=jax.ShapeDtypeStruct((SEQ, SWA_WIDTH), BF16),
        scratch_shapes=[pltpu.VMEM((SWA_KV_HEADS, SWA_KV_WIDTH, SWA_GROUP * LANES), BF16)],
        compiler_params=_params("arbitrary"),
        name="swa_attn",
    )(slopes, sinks, proj_t, proj, proj_t)


def _mix_body(a_ref, b_ref, ga_ref, gb_ref, wa_ref, wb_ref, wo_ref, x_ref, gpost_ref, o_ref, acc_scr):
    j = pl.program_id(1)

    @pl.when(j == 0)
    def _():
        acc_scr[...] = jnp.zeros_like(acc_scr)

    ya = _dot(a_ref[...], wa_ref[...])
    yb = _dot(b_ref[...], wb_ref[...])
    mixed = ga_ref[...].astype(F32) * ya + gb_ref[...].astype(F32) * yb
    acc_scr[...] += _dot(mixed.astype(BF16), wo_ref[...])

    @pl.when(j == pl.num_programs(1) - 1)
    def _():
        o_ref[...] = x_ref[...] + _rms(acc_scr[...]) * gpost_ref[...]


def _mix_out(a, b, proj, wa, wb, wo, x, gpost, *, tm=512, tn=512):
    s, d = x.shape
    nj = d // tn
    return pl.pallas_call(
        _mix_body,
        grid=(s // tm, nj),
        in_specs=[
            pl.BlockSpec((tm, DIFF_WIDTH), lambda i, j: (i, 0)),
            pl.BlockSpec((tm, SWA_WIDTH), lambda i, j: (i, 0)),
            pl.BlockSpec((tm, tn), lambda i, j: (i, j)),
            pl.BlockSpec((tm, tn), lambda i, j: (i, nj + j)),
            pl.BlockSpec((DIFF_WIDTH, tn), lambda i, j: (0, j)),
            pl.BlockSpec((SWA_WIDTH, tn), lambda i, j: (0, j)),
            pl.BlockSpec((tn, d), lambda i, j: (j, 0)),
            pl.BlockSpec((tm, d), lambda i, j: (i, 0)),
            pl.BlockSpec((1, d), lambda i, j: (0, 0)),
        ],
        out_specs=pl.BlockSpec((tm, d), lambda i, j: (i, 0)),
        out_shape=jax.ShapeDtypeStruct((s, d), F32),
        scratch_shapes=[pltpu.VMEM((tm, d), F32)],
        compiler_params=_params("arbitrary", "arbitrary"),
        name="mix_out",
    )(a, b, proj, proj, wa, wb, wo, x, gpost)


def _alibi_slopes(n):
    return jnp.asarray(2.0 ** (-8.0 * np.arange(1, n + 1) / n), dtype=F32)


def _ffn_weights(w_in, w_out):
    pad = D_FF_PAD - D_FF
    wg = jnp.pad(w_in[:, :D_FF], ((0, 0), (0, pad))).astype(BF16)
    wu = jnp.pad(w_in[:, D_FF:], ((0, 0), (0, pad))).astype(BF16)
    wo = jnp.pad(w_out, ((0, pad), (0, 0))).astype(BF16)
    return wg, wu, wo


def kernel(x, ffn1_norm_pre, ffn1_w_in, ffn1_w_out, ffn1_norm_post, mix_norm_pre, w_in, diff_lambda, diff_subln, swa_sinks, w_branch_diff, w_branch_swa, w_out, mix_norm_post, ffn2_norm_pre, ffn2_w_in, ffn2_w_out, ffn2_norm_post):
    depth = ffn1_w_in.shape[0]
    xs = x.reshape(SEQ, D_MODEL)
    for l in range(depth):
        lam_init = 0.8 - 0.6 * math.exp(-0.3 * l)
        w = w_in[l]
        o_dq, o_dk, o_dv = 0, DIFF_WIDTH, 2 * DIFF_WIDTH
        o_sq = 3 * DIFF_WIDTH
        o_sk = o_sq + SWA_WIDTH
        o_sv = o_sk + SWA_KV_WIDTH
        o_g = o_sv + SWA_KV_WIDTH
        wa = jnp.concatenate([w[:, o_g:], w[:, o_dk:o_dv], w[:, o_sk:o_sv]], axis=1).astype(BF16)
        wbt = jnp.concatenate([w[:, o_dq:o_dk], w[:, o_dv:o_sq], w[:, o_sq:o_sk], w[:, o_sv:o_g]],
                              axis=1).T.astype(BF16)
        wbs = (w_branch_swa[l].reshape(2, 2, SWA_GROUP, SWA_HEAD_DIM, D_MODEL)
               .transpose(0, 2, 1, 3, 4).reshape(SWA_WIDTH, D_MODEL).astype(BF16))
        wbd = w_branch_diff[l].astype(BF16)
        wo = w_out[l].astype(BF16)
        row = lambda v: v.reshape(1, -1)

        wg1, wu1, wo1 = _ffn_weights(ffn1_w_in[l], ffn1_w_out[l])
        x1, h2 = _ffn(xs, row(ffn1_norm_pre[l]), wg1, wu1, wo1, row(ffn1_norm_post[l]), row(mix_norm_pre[l]))
        proj = _proj(h2, wa)
        proj_t = _proj_t(wbt, h2)
        a = _diff_attn(_alibi_slopes(DIFF_HEADS) * LOG2E, diff_lambda[l], row(diff_subln[l]), proj, proj_t,
                       lam_init=lam_init)
        b = _swa_attn(_alibi_slopes(SWA_Q_HEADS), swa_sinks[l], proj, proj_t)
        x2 = _mix_out(a, b, proj, wbd, wbs, wo, x1, row(mix_norm_post[l]))
        wg2, wu2, wo2 = _ffn_weights(ffn2_w_in[l], ffn2_w_out[l])
        xs = _ffn(x2, row(ffn2_norm_pre[l]), wg2, wu2, wo2, row(ffn2_norm_post[l]))
    return xs.reshape(x.shape)
```

```python
import functools
import math

import numpy as np
import jax
import jax.numpy as jnp
from jax import lax
from jax.experimental import pallas as pl
from jax.experimental.pallas import tpu as pltpu

D_MODEL = 2048
SEQ = 8192
DIFF_HEADS = 8
DIFF_HEAD_DIM = 64
DIFF_V_DIM = 128
DIFF_WIDTH = 1024
SWA_Q_HEADS = 16
SWA_KV_HEADS = 4
SWA_GROUP = 4
SWA_HEAD_DIM = 64
SWA_WIDTH = 1024
SWA_KV_WIDTH = 256
WINDOW = 128
D_FF = 5504
NORM_EPS = 1e-6
NEG_INF = -1e30

LANES = 128
MXU_DIM = 256
VMEM_LIMIT_BYTES = 56 * 1024 * 1024

D_FF_PAD = ((D_FF + 2 * MXU_DIM - 1) // (2 * MXU_DIM)) * (2 * MXU_DIM)
N_GATE = 2 * D_MODEL
PROJ_COLS = N_GATE + DIFF_WIDTH + SWA_KV_WIDTH
PROJT_ROWS = DIFF_WIDTH + DIFF_WIDTH + SWA_WIDTH + SWA_KV_WIDTH
N_SLABS = SEQ // LANES

BF16 = jnp.bfloat16
F32 = jnp.float32


def _dot(a, b):
    return jnp.dot(a, b, preferred_element_type=F32)


def _rms(x):
    return x * lax.rsqrt(jnp.mean(x * x, axis=-1, keepdims=True) + NORM_EPS)


def _params(*sem, flags=None):
    return pltpu.CompilerParams(dimension_semantics=sem, vmem_limit_bytes=VMEM_LIMIT_BYTES, flags=flags)


def _ffn_body(*refs, emit_next, overlap):
    if emit_next:
        (x_ref, gpre_ref, wg_ref, wu_ref, wo_ref, gpost_ref, gnext_ref,
         o_ref, hn_ref, h_scr, acc_scr) = refs
    else:
        (x_ref, gpre_ref, wg_ref, wu_ref, wo_ref, gpost_ref,
         o_ref, h_scr, acc_scr) = refs
    j = pl.program_id(1)
    last = pl.num_programs(1) - 1

    @pl.when(j == 0)
    def _():
        h_scr[...] = (_rms(x_ref[...]) * gpre_ref[...]).astype(BF16)
        acc_scr[...] = jnp.zeros_like(acc_scr)

    h = h_scr[...]
    g = _dot(h, wg_ref[...])
    u = _dot(h, wu_ref[...])
    act = jax.nn.silu(g) * u
    if overlap:
        col = lax.broadcasted_iota(jnp.int32, act.shape, 1)
        act = jnp.where((j == last) & (col < overlap), 0.0, act)
    acc_scr[...] += _dot(act.astype(BF16), wo_ref[...])

    @pl.when(j == last)
    def _():
        x1 = x_ref[...] + 0.5 * (_rms(acc_scr[...]) * gpost_ref[...])
        o_ref[...] = x1
        if emit_next:
            hn_ref[...] = (_rms(x1) * gnext_ref[...]).astype(BF16)


def _ffn(x, gpre, w_in, w_out, gpost, gnext=None, *, tm=512, tf=512):
    emit_next = gnext is not None
    s, d = x.shape
    d_ff = w_out.shape[0]
    nj = pl.cdiv(d_ff, tf)
    overlap = nj * tf - d_ff
    assert tf % LANES == 0 and d_ff % LANES == 0
    start = lambda j, base=0: (jnp.minimum(j * (tf // LANES), (d_ff - tf) // LANES) + base // LANES) * LANES
    row = lambda i, j: (i, 0)
    fixed = lambda i, j: (0, 0)
    in_specs = [
        pl.BlockSpec((tm, d), row),
        pl.BlockSpec((1, d), fixed),
        pl.BlockSpec((pl.Element(d), pl.Element(tf)), lambda i, j: (0, start(j))),
        pl.BlockSpec((pl.Element(d), pl.Element(tf)), lambda i, j: (0, start(j, d_ff))),
        pl.BlockSpec((pl.Element(tf), pl.Element(d)), lambda i, j: (start(j), 0)),
        pl.BlockSpec((1, d), fixed),
    ]
    args = [x, gpre, w_in, w_in, w_out, gpost]
    out_shape = [jax.ShapeDtypeStruct((s, d), F32)]
    out_specs = [pl.BlockSpec((tm, d), row)]
    if emit_next:
        in_specs.append(pl.BlockSpec((1, d), fixed))
        args.append(gnext)
        out_shape.append(jax.ShapeDtypeStruct((s, d), BF16))
        out_specs.append(pl.BlockSpec((tm, d), row))
    res = pl.pallas_call(
        functools.partial(_ffn_body, emit_next=emit_next, overlap=overlap),
        grid=(s // tm, nj),
        in_specs=in_specs,
        out_specs=out_specs,
        out_shape=out_shape,
        scratch_shapes=[pltpu.VMEM((tm, d), BF16), pltpu.VMEM((tm, d), F32)],
        compiler_params=_params("arbitrary", "arbitrary"),
        name="ffn_next" if emit_next else "ffn",
    )(*args)
    return res if emit_next else res[0]


PROJ_TN = MXU_DIM
_O_DQ, _O_DK, _O_DV, _O_SQ = (k * DIFF_WIDTH // PROJ_TN for k in range(4))
_O_SK = _O_SQ + SWA_WIDTH // PROJ_TN
_O_SV = _O_SK + SWA_KV_WIDTH // PROJ_TN
_O_G = _O_SV + SWA_KV_WIDTH // PROJ_TN
_N_G, _N_D, _N_S = N_GATE // PROJ_TN, DIFF_WIDTH // PROJ_TN, SWA_WIDTH // PROJ_TN


def _proj_src_tile(t):
    return jnp.where(t < _N_G, _O_G + t, jnp.where(t < _N_G + _N_D, _O_DK + (t - _N_G), _O_SK))


def _proj_t_src_tile(t):
    return jnp.where(t < _N_D, _O_DQ + t,
                     jnp.where(t < 2 * _N_D, _O_DV + (t - _N_D),
                               jnp.where(t < 2 * _N_D + _N_S, _O_SQ + (t - 2 * _N_D), _O_SV)))


def _proj_body(h_ref, w_ref, o_ref):
    res = _dot(h_ref[...], w_ref[...])
    o_ref[...] = jnp.where(pl.program_id(1) < _N_G, jax.nn.sigmoid(res), res).astype(BF16)


def _proj(h, w, *, tm=2048):
    s, d = h.shape
    tn = PROJ_TN
    return pl.pallas_call(
        _proj_body,
        grid=(s // tm, PROJ_COLS // tn),
        in_specs=[pl.BlockSpec((tm, d), lambda i, t: (i, 0)),
                  pl.BlockSpec((d, tn), lambda i, t: (0, _proj_src_tile(t)))],
        out_specs=pl.BlockSpec((tm, tn), lambda i, t: (i, t)),
        out_shape=jax.ShapeDtypeStruct((s, PROJ_COLS), BF16),
        compiler_params=_params("arbitrary", "arbitrary"),
        name="proj",
    )(h, w)


def _proj_t_body(h_ref, w_ref, o_ref):
    res = _dot(h_ref[...], w_ref[...])
    for sl in range(o_ref.shape[0]):
        o_ref[sl] = res[sl * LANES:(sl + 1) * LANES, :].T.astype(BF16)


def _proj_t(h, w, *, tm=2048):
    s, d = h.shape
    tn = PROJ_TN
    return pl.pallas_call(
        _proj_t_body,
        grid=(s // tm, PROJT_ROWS // tn),
        in_specs=[pl.BlockSpec((tm, d), lambda i, t: (i, 0)),
                  pl.BlockSpec((d, tn), lambda i, t: (0, _proj_t_src_tile(t)))],
        out_specs=pl.BlockSpec((tm // LANES, tn, LANES), lambda i, t: (i, t, 0)),
        out_shape=jax.ShapeDtypeStruct((s // LANES, PROJT_ROWS, LANES), BF16),
        compiler_params=_params("arbitrary", "arbitrary"),
        name="proj_t",
    )(h, w)


DIFF_TQ = 512
DIFF_TK = 256
DIFF_ACC_ROWS = DIFF_V_DIM + 16
LOG2E = math.log2(math.e)


DIFF_HPS = 2


def _diff_body(slopes_ref, lam_ref, q_ref, k_ref, v_ref, e_ref, g_ref, o_ref,
               qaug_scr, z0_scr, z1_scr, zmax0_scr, zmax1_scr, m_scr, acc_scr, *, lam_init):
    tq, tk, nh = DIFF_TQ, DIFF_TK, DIFF_TQ // DIFF_TK
    hp = pl.program_id(0)
    i = pl.program_id(1)
    z_scrs = (z0_scr, z1_scr)
    zmax_scrs = (zmax0_scr, zmax1_scr)
    slopes = [slopes_ref[hp * DIFF_HPS + hh] for hh in range(DIFF_HPS)]

    row = lax.broadcasted_iota(jnp.int32, (LANES, tq), 0)
    for hh in range(DIFF_HPS):
        q = jnp.concatenate([q_ref[s, hh * LANES:(hh + 1) * LANES, :] for s in range(tq // LANES)], axis=1)
        qs = (q.astype(F32) * (DIFF_HEAD_DIM ** -0.5 * LOG2E)).astype(BF16)
        zero = jnp.zeros_like(qs)
        s_f32 = jnp.where(row < 2, slopes[hh], jnp.where(row < 4, slopes[hh] * tk, 0.0))
        s_hi = s_f32.astype(BF16)
        s_lo = (s_f32 - s_hi.astype(F32)).astype(BF16)
        srow = jnp.where(row % 2 == 0, s_hi, s_lo)
        qaug_scr[2 * hh, 0:LANES, :] = jnp.where(row < DIFF_HEAD_DIM, qs, zero)
        qaug_scr[2 * hh + 1, 0:LANES, :] = jnp.where(row >= DIFF_HEAD_DIM, qs, zero)
        qaug_scr[2 * hh, LANES:2 * LANES, :] = srow
        qaug_scr[2 * hh + 1, LANES:2 * LANES, :] = srow
    m_scr[...] = jnp.full_like(m_scr, NEG_INF)
    acc_scr[...] = jnp.zeros_like(acc_scr)

    def qk(hh, j):
        for s in range(nh):
            rows = pl.ds(pl.multiple_of(j * tq + s * tk, tk), tk)
            kc = jnp.concatenate([k_ref[rows, hh * LANES:(hh + 1) * LANES],
                                  e_ref[s * tk:(s + 1) * tk, :]], axis=1)
            for c in range(2):
                z = _dot(kc, qaug_scr[2 * hh + c])
                z_scrs[hh][c * nh + s] = z
                zmax_scrs[hh][c * nh + s] = jnp.max(z, axis=0, keepdims=True)

    def softmax_pv(hh, j, diagonal):
        vt = jnp.concatenate([v_ref[(tq // LANES) * j + s, hh * LANES:(hh + 1) * LANES, :]
                              for s in range(tq // LANES)], axis=1)
        vt = jnp.concatenate([vt, jnp.ones((DIFF_ACC_ROWS - DIFF_V_DIM, tq), BF16)], axis=0)
        koff = (jnp.zeros((1, tq), jnp.int32) + j * tq).astype(F32) * slopes[hh]
        kr = lax.broadcasted_iota(jnp.int32, (tk, tq), 0)
        qc = lax.broadcasted_iota(jnp.int32, (tk, tq), 1)
        for c in range(2):
            zs = [z_scrs[hh][c * nh + s] for s in range(nh)]
            if diagonal:
                zs = [jnp.where(kr + s * tk <= qc, z, NEG_INF) for s, z in enumerate(zs)]
                zmaxs = [jnp.max(z, axis=0, keepdims=True) for z in zs]
            else:
                zmaxs = [zmax_scrs[hh][c * nh + s] for s in range(nh)]
            m_old = m_scr[2 * hh + c]
            m_tile = functools.reduce(jnp.maximum, zmaxs)
            m_new = jnp.maximum(m_old, m_tile + koff)
            alpha = jnp.exp2(m_old - m_new)
            shift = m_new - koff
            p = jnp.concatenate([jnp.exp2(z - shift).astype(BF16) for z in zs], axis=0)
            acc_scr[2 * hh + c] = alpha * acc_scr[2 * hh + c] + _dot(vt, p)
            m_scr[2 * hh + c] = m_new

    qk(0, 0)

    def body(j, carry):
        qk(1, j)
        softmax_pv(0, j, False)
        qk(0, j + 1)
        softmax_pv(1, j, False)
        return carry

    lax.fori_loop(0, i, body, 0)
    qk(1, i)
    softmax_pv(0, i, True)
    softmax_pv(1, i, True)

    lp = lam_ref[...]
    lam = (jnp.exp(jnp.sum(lp[0:1, :] * lp[1:2, :], axis=1, keepdims=True))
           - jnp.exp(jnp.sum(lp[2:3, :] * lp[3:4, :], axis=1, keepdims=True)) + lam_init)
    for hh in range(DIFF_HPS):
        a0, a1 = acc_scr[2 * hh], acc_scr[2 * hh + 1]
        o0 = a0[0:DIFF_V_DIM, :] * (1.0 / a0[DIFF_V_DIM:DIFF_V_DIM + 1, :])
        o1 = a1[0:DIFF_V_DIM, :] * (1.0 / a1[DIFF_V_DIM:DIFF_V_DIM + 1, :])
        o = o0 - lam * o1
        y = o * lax.rsqrt(jnp.mean(o * o, axis=0, keepdims=True) + NORM_EPS)
        o_ref[:, hh * DIFF_V_DIM:(hh + 1) * DIFF_V_DIM] = ((y.T * g_ref[...]) * (1.0 - lam_init)).astype(BF16)


def _diff_attn(slopes, lam_params, subln, proj, proj_t, *, lam_init):
    tq, tk, hps = DIFF_TQ, DIFF_TK, DIFF_HPS
    w = hps * LANES
    k_col0 = N_GATE // w
    v_row0 = DIFF_WIDTH // w
    r = np.arange(tq)[:, None]
    c = np.arange(LANES)[None, :]
    e = jnp.asarray(np.where(c < 2, r % tk, np.where(c < 4, r // tk, 0)), dtype=BF16)
    return pl.pallas_call(
        functools.partial(_diff_body, lam_init=lam_init),
        grid=(DIFF_HEADS // hps, SEQ // tq),
        in_specs=[
            pl.BlockSpec(memory_space=pltpu.SMEM),
            pl.BlockSpec((4, DIFF_HEAD_DIM), lambda h, i: (0, 0)),
            pl.BlockSpec((tq // LANES, w, LANES), lambda h, i: (i, h, 0)),
            pl.BlockSpec((SEQ, w), lambda h, i: (0, k_col0 + h)),
            pl.BlockSpec((N_SLABS, w, LANES), lambda h, i: (0, v_row0 + h, 0)),
            pl.BlockSpec((tq, LANES), lambda h, i: (0, 0)),
            pl.BlockSpec((1, DIFF_V_DIM), lambda h, i: (0, 0)),
        ],
        out_specs=pl.BlockSpec((tq, hps * DIFF_V_DIM), lambda h, i: (i, h)),
        out_shape=jax.ShapeDtypeStruct((SEQ, DIFF_WIDTH), BF16),
        scratch_shapes=[
            pltpu.VMEM((2 * hps, 2 * LANES, tq), BF16),
            pltpu.VMEM((2 * (tq // tk), tk, tq), F32),
            pltpu.VMEM((2 * (tq // tk), tk, tq), F32),
            pltpu.VMEM((2 * (tq // tk), 1, tq), F32),
            pltpu.VMEM((2 * (tq // tk), 1, tq), F32),
            pltpu.VMEM((2 * hps, 1, tq), F32),
            pltpu.VMEM((2 * hps, DIFF_ACC_ROWS, tq), F32),
        ],
        compiler_params=_params("arbitrary", "arbitrary"),
        name="diff_attn",
    )(slopes, lam_params, proj_t, proj, proj_t, e, subln)


SWA_TQ = 512
SWA_BLOCKS = SWA_TQ // WINDOW


def _swa_body(slopes_ref, sinks_ref, q_ref, k_ref, v_ref, o_ref, qaug_scr):
    i = pl.program_id(0)

    @pl.when(i == 0)
    def _():
        qaug_scr[...] = jnp.zeros_like(qaug_scr)

    def block(t, carry):
        n = i * SWA_BLOCKS + t
        first = n == 0
        off = jnp.where(first, 0, WINDOW)
        slab0 = jnp.where(first, 0, n - 1)
        kwin = k_ref[pl.ds(pl.multiple_of(slab0 * WINDOW, WINDOW), 2 * WINDOW), :]
        vt = jnp.concatenate([v_ref[slab0], v_ref[slab0 + 1]], axis=1)
        qs = q_ref[t]
        kb = lax.broadcasted_iota(jnp.int32, (2 * WINDOW, WINDOW), 0)
        qa = lax.broadcasted_iota(jnp.int32, (2 * WINDOW, WINDOW), 1)
        dist = off + qa - kb
        valid = (dist >= 0) & (dist < WINDOW)
        distf = dist.astype(F32)
        outs = []
        for kh in range(SWA_KV_HEADS):
            r0 = kh * SWA_HEAD_DIM
            for g in range(SWA_GROUP):
                hd = kh * SWA_GROUP + g
                qh = qs[hd * SWA_HEAD_DIM:(hd + 1) * SWA_HEAD_DIM, :].astype(F32) * (SWA_HEAD_DIM ** -0.5)
                qaug_scr[kh, r0:r0 + SWA_HEAD_DIM, g * LANES:(g + 1) * LANES] = qh.astype(BF16)
            z = _dot(kwin, qaug_scr[kh])
            vt_kh = vt[r0:r0 + SWA_HEAD_DIM, :]
            row = []
            for g in range(SWA_GROUP):
                hd = kh * SWA_GROUP + g
                sink = sinks_ref[hd]
                sc = jnp.where(valid, z[:, g * LANES:(g + 1) * LANES] - slopes_ref[hd] * distf, NEG_INF)
                m = jnp.maximum(jnp.max(sc, axis=0, keepdims=True), sink)
                p = jnp.exp(sc - m)
                denom = jnp.sum(p, axis=0, keepdims=True) + jnp.exp(sink - m)
                row.append(_dot(vt_kh, p.astype(BF16)) * (1.0 / denom))
            outs.append(row)
        r = pl.multiple_of(t * WINDOW, WINDOW)
        for pr in range(SWA_KV_HEADS // 2):
            for g in range(SWA_GROUP):
                x = jnp.concatenate([outs[2 * pr][g], outs[2 * pr + 1][g]], axis=0)
                c0 = (pr * SWA_GROUP + g) * LANES
                o_ref[pl.ds(r, WINDOW), c0:c0 + LANES] = x.T.astype(BF16)
        return carry

    lax.fori_loop(0, SWA_BLOCKS, block, 0)


def _swa_attn(slopes, sinks, proj, proj_t):
    q_row_blk = (2 * DIFF_WIDTH) // SWA_WIDTH
    k_col_blk = (N_GATE + DIFF_WIDTH) // SWA_KV_WIDTH
    v_row_blk = (2 * DIFF_WIDTH + SWA_WIDTH) // SWA_KV_WIDTH
    return pl.pallas_call(
        _swa_body,
        grid=(SEQ // SWA_TQ,),
        in_specs=[
            pl.BlockSpec(memory_space=pltpu.SMEM),
            pl.BlockSpec(memory_space=pltpu.SMEM),
            pl.BlockSpec((SWA_BLOCKS, SWA_WIDTH, LANES), lambda i: (i, q_row_blk, 0)),
            pl.BlockSpec((SEQ, SWA_KV_WIDTH), lambda i: (0, k_col_blk)),
            pl.BlockSpec((N_SLABS, SWA_KV_WIDTH, LANES), lambda i: (0, v_row_blk, 0)),
        ],
        out_specs=pl.BlockSpec((SWA_TQ, SWA_WIDTH), lambda i: (i, 0)),
        out_shape=jax.ShapeDtypeStruct((SEQ, SWA_WIDTH), BF16),
        scratch_shapes=[pltpu.VMEM((SWA_KV_HEADS, SWA_KV_WIDTH, SWA_GROUP * LANES), BF16)],
        compiler_params=_params("arbitrary"),
        name="swa_attn",
    )(slopes, sinks, proj_t, proj, proj_t)


def _mix_body(a_ref, b_ref, ga_ref, gb_ref, wa_ref, wb_ref, wo_ref, x_ref, gpost_ref, o_ref, acc_scr):
    j = pl.program_id(1)

    @pl.when(j == 0)
    def _():
        acc_scr[...] = jnp.zeros_like(acc_scr)

    ya = _dot(a_ref[...], wa_ref[...])
    yb = _dot(b_ref[...], wb_ref[...])
    mixed = ga_ref[...].astype(F32) * ya + gb_ref[...].astype(F32) * yb
    acc_scr[...] += _dot(mixed.astype(BF16), wo_ref[...])

    @pl.when(j == pl.num_programs(1) - 1)
    def _():
        o_ref[...] = x_ref[...] + _rms(acc_scr[...]) * gpost_ref[...]


def _mix_out(a, b, proj, wa, wb, wo, x, gpost, *, tm=512, tn=512):
    s, d = x.shape
    nj = d // tn
    return pl.pallas_call(
        _mix_body,
        grid=(s // tm, nj),
        in_specs=[
            pl.BlockSpec((tm, DIFF_WIDTH), lambda i, j: (i, 0)),
            pl.BlockSpec((tm, SWA_WIDTH), lambda i, j: (i, 0)),
            pl.BlockSpec((tm, tn), lambda i, j: (i, j)),
            pl.BlockSpec((tm, tn), lambda i, j: (i, nj + j)),
            pl.BlockSpec((DIFF_WIDTH, tn), lambda i, j: (0, j)),
            pl.BlockSpec((SWA_WIDTH, tn), lambda i, j: (0, j)),
            pl.BlockSpec((tn, d), lambda i, j: (j, 0)),
            pl.BlockSpec((tm, d), lambda i, j: (i, 0)),
            pl.BlockSpec((1, d), lambda i, j: (0, 0)),
        ],
        out_specs=pl.BlockSpec((tm, d), lambda i, j: (i, 0)),
        out_shape=jax.ShapeDtypeStruct((s, d), F32),
        scratch_shapes=[pltpu.VMEM((tm, d), F32)],
        compiler_params=_params("arbitrary", "arbitrary"),
        name="mix_out",
    )(a, b, proj, proj, wa, wb, wo, x, gpost)


def _alibi_slopes(n):
    return jnp.asarray(2.0 ** (-8.0 * np.arange(1, n + 1) / n), dtype=F32)


def kernel(x, ffn1_norm_pre, ffn1_w_in, ffn1_w_out, ffn1_norm_post, mix_norm_pre, w_in, diff_lambda, diff_subln, swa_sinks, w_branch_diff, w_branch_swa, w_out, mix_norm_post, ffn2_norm_pre, ffn2_w_in, ffn2_w_out, ffn2_norm_post):
    depth = ffn1_w_in.shape[0]
    xs = x.reshape(SEQ, D_MODEL)
    for l in range(depth):
        lam_init = 0.8 - 0.6 * math.exp(-0.3 * l)
        w = w_in[l].astype(BF16)
        wbs = (w_branch_swa[l].reshape(2, 2, SWA_GROUP, SWA_HEAD_DIM, D_MODEL)
               .transpose(0, 2, 1, 3, 4).reshape(SWA_WIDTH, D_MODEL).astype(BF16))
        wbd = w_branch_diff[l].astype(BF16)
        wo = w_out[l].astype(BF16)
        row = lambda v: v.reshape(1, -1)

        x1, h2 = _ffn(xs, row(ffn1_norm_pre[l]), ffn1_w_in[l].astype(BF16), ffn1_w_out[l].astype(BF16),
                      row(ffn1_norm_post[l]), row(mix_norm_pre[l]))
        proj = _proj(h2, w)
        proj_t = _proj_t(h2, w)
        a = _diff_attn(_alibi_slopes(DIFF_HEADS) * LOG2E, diff_lambda[l], row(diff_subln[l]), proj, proj_t,
                       lam_init=lam_init)
        b = _swa_attn(_alibi_slopes(SWA_Q_HEADS), swa_sinks[l], proj, proj_t)
        x2 = _mix_out(a, b, proj, wbd, wbs, wo, x1, row(mix_norm_post[l]))
        xs = _ffn(x2, row(ffn2_norm_pre[l]), ffn2_w_in[l].astype(BF16), ffn2_w_out[l].astype(BF16),
                  row(ffn2_norm_post[l]))
    return xs.reshape(x.shape)
```

```python
import functools
import math

import numpy as np
import jax
import jax.numpy as jnp
from jax import lax
from jax.experimental import pallas as pl
from jax.experimental.pallas import tpu as pltpu

D_MODEL = 2048
SEQ = 8192
DIFF_HEADS = 8
DIFF_HEAD_DIM = 64
DIFF_V_DIM = 128
DIFF_WIDTH = 1024
SWA_Q_HEADS = 16
SWA_KV_HEADS = 4
SWA_GROUP = 4
SWA_HEAD_DIM = 64
SWA_WIDTH = 1024
SWA_KV_WIDTH = 256
WINDOW = 128
D_FF = 5504
NORM_EPS = 1e-6
NEG_INF = -1e30

LANES = 128
MXU_DIM = 256
VMEM_LIMIT_BYTES = 60 * 1024 * 1024

N_GATE = 2 * D_MODEL
PROJ_COLS = N_GATE + DIFF_WIDTH + 2 * SWA_KV_WIDTH
PROJT_ROWS = DIFF_WIDTH + DIFF_WIDTH + SWA_WIDTH + 2 * SWA_KV_WIDTH
N_SLABS = SEQ // LANES

BF16 = jnp.bfloat16
F32 = jnp.float32


def _dot(a, b):
    return jnp.dot(a, b, preferred_element_type=F32)


def _rms(x):
    return x * lax.rsqrt(jnp.mean(x * x, axis=-1, keepdims=True) + NORM_EPS)


def _params(*sem, flags=None):
    return pltpu.CompilerParams(dimension_semantics=sem, vmem_limit_bytes=VMEM_LIMIT_BYTES, flags=flags)


FFN_ROW_CHUNK = 512


def _ffn_body(*refs, emit_next, overlap):
    if emit_next:
        (x_ref, gpre_ref, wg_ref, wu_ref, wo_ref, gpost_ref, gnext_ref,
         o_ref, hn_ref, h_scr) = refs
    else:
        (x_ref, gpre_ref, wg_ref, wu_ref, wo_ref, gpost_ref,
         o_ref, h_scr) = refs
    j = pl.program_id(1)
    last = pl.num_programs(1) - 1

    @pl.when(j == 0)
    def _():
        h_scr[...] = (_rms(x_ref[...]) * gpre_ref[...]).astype(BF16)
        o_ref[...] = jnp.zeros_like(o_ref)

    for r in range(0, h_scr.shape[0], FFN_ROW_CHUNK):
        rows = slice(r, r + FFN_ROW_CHUNK)
        h = h_scr[rows, :]
        g = _dot(h, wg_ref[...])
        u = _dot(h, wu_ref[...])
        act = jax.nn.silu(g) * u
        if overlap:
            col = lax.broadcasted_iota(jnp.int32, act.shape, 1)
            act = jnp.where((j == last) & (col < overlap), 0.0, act)
        o_ref[rows, :] += _dot(act.astype(BF16), wo_ref[...])

    @pl.when(j == last)
    def _():
        x1 = x_ref[...] + 0.5 * (_rms(o_ref[...]) * gpost_ref[...])
        o_ref[...] = x1
        if emit_next:
            hn_ref[...] = (_rms(x1) * gnext_ref[...]).astype(BF16)


def _ffn(x, gpre, w_in, w_out, gpost, gnext=None, *, tf=512):
    emit_next = gnext is not None
    tm = 512 if emit_next else 1024
    s, d = x.shape
    d_ff = w_out.shape[0]
    nj = pl.cdiv(d_ff, tf)
    overlap = nj * tf - d_ff
    assert tf % LANES == 0 and d_ff % LANES == 0
    start = lambda j, base=0: (jnp.minimum(j * (tf // LANES), (d_ff - tf) // LANES) + base // LANES) * LANES
    row = lambda i, j: (i, 0)
    fixed = lambda i, j: (0, 0)
    in_specs = [
        pl.BlockSpec((tm, d), row),
        pl.BlockSpec((1, d), fixed),
        pl.BlockSpec((pl.Element(d), pl.Element(tf)), lambda i, j: (0, start(j))),
        pl.BlockSpec((pl.Element(d), pl.Element(tf)), lambda i, j: (0, start(j, d_ff))),
        pl.BlockSpec((pl.Element(tf), pl.Element(d)), lambda i, j: (start(j), 0)),
        pl.BlockSpec((1, d), fixed),
    ]
    args = [x, gpre, w_in, w_in, w_out, gpost]
    out_shape = [jax.ShapeDtypeStruct((s, d), F32)]
    out_specs = [pl.BlockSpec((tm, d), row)]
    if emit_next:
        in_specs.append(pl.BlockSpec((1, d), fixed))
        args.append(gnext)
        out_shape.append(jax.ShapeDtypeStruct((s, d), BF16))
        out_specs.append(pl.BlockSpec((tm, d), row))
    res = pl.pallas_call(
        functools.partial(_ffn_body, emit_next=emit_next, overlap=overlap),
        grid=(s // tm, nj),
        in_specs=in_specs,
        out_specs=out_specs,
        out_shape=out_shape,
        scratch_shapes=[pltpu.VMEM((tm, d), BF16)],
        compiler_params=_params("arbitrary", "arbitrary"),
        name="ffn_next" if emit_next else "ffn",
    )(*args)
    return res if emit_next else res[0]


PROJ_TN = 2 * MXU_DIM
assert 2 * SWA_KV_WIDTH == PROJ_TN
_O_DQ, _O_DK, _O_DV, _O_SQ = (k * DIFF_WIDTH // PROJ_TN for k in range(4))
_O_SKV = _O_SQ + SWA_WIDTH // PROJ_TN
_O_G = _O_SKV + 1
_N_G, _N_D, _N_S = N_GATE // PROJ_TN, DIFF_WIDTH // PROJ_TN, SWA_WIDTH // PROJ_TN


def _proj_src_tile(t):
    return jnp.where(t < _N_G, _O_G + t, jnp.where(t < _N_G + _N_D, _O_DK + (t - _N_G), _O_SKV))


def _proj_t_src_tile(t):
    return jnp.where(t < _N_D, _O_DQ + t,
                     jnp.where(t < 2 * _N_D, _O_DV + (t - _N_D),
                               jnp.where(t < 2 * _N_D + _N_S, _O_SQ + (t - 2 * _N_D), _O_SKV)))


PROJ_ROW_CHUNK = 256


def _proj_body(h_ref, w_ref, o_ref):
    def run(gated):
        for r in range(0, h_ref.shape[0], PROJ_ROW_CHUNK):
            rows = slice(r, r + PROJ_ROW_CHUNK)
            res = _dot(h_ref[rows, :], w_ref[...])
            if gated:
                res = 0.5 * jnp.tanh(0.5 * res) + 0.5
            o_ref[rows, :] = res.astype(BF16)

    gated = pl.program_id(1) < _N_G
    pl.when(gated)(lambda: run(True))
    pl.when(jnp.logical_not(gated))(lambda: run(False))


def _proj(h, w, *, tm=2048):
    s, d = h.shape
    tn = PROJ_TN
    return pl.pallas_call(
        _proj_body,
        grid=(s // tm, PROJ_COLS // tn),
        in_specs=[pl.BlockSpec((tm, d), lambda i, t: (i, 0)),
                  pl.BlockSpec((d, tn), lambda i, t: (0, _proj_src_tile(t)))],
        out_specs=pl.BlockSpec((tm, tn), lambda i, t: (i, t)),
        out_shape=jax.ShapeDtypeStruct((s, PROJ_COLS), BF16),
        compiler_params=_params("arbitrary", "arbitrary"),
        name="proj",
    )(h, w)


def _proj_t_body(h_ref, w_ref, o_ref):
    for r in range(0, h_ref.shape[0], PROJ_ROW_CHUNK):
        res = _dot(h_ref[r:r + PROJ_ROW_CHUNK, :], w_ref[...])
        for k in range(PROJ_ROW_CHUNK // LANES):
            o_ref[r // LANES + k] = res[k * LANES:(k + 1) * LANES, :].T.astype(BF16)


def _proj_t(h, w, *, tm=2048):
    s, d = h.shape
    tn = PROJ_TN
    return pl.pallas_call(
        _proj_t_body,
        grid=(s // tm, PROJT_ROWS // tn),
        in_specs=[pl.BlockSpec((tm, d), lambda i, t: (i, 0)),
                  pl.BlockSpec((d, tn), lambda i, t: (0, _proj_t_src_tile(t)))],
        out_specs=pl.BlockSpec((tm // LANES, tn, LANES), lambda i, t: (i, t, 0)),
        out_shape=jax.ShapeDtypeStruct((s // LANES, PROJT_ROWS, LANES), BF16),
        compiler_params=_params("arbitrary", "arbitrary"),
        name="proj_t",
    )(h, w)


DIFF_TQ = 512
DIFF_TK = 256
DIFF_ACC_ROWS = DIFF_V_DIM + 16
LOG2E = math.log2(math.e)


DIFF_HPS = 2


def _diff_body(slopes_ref, lam_ref, q_ref, k_ref, v_ref, e_ref, g_ref, o_ref,
               qaug_scr, z0_scr, z1_scr, zmax0_scr, zmax1_scr, m_scr, acc_scr, *, lam_init):
    tq, tk, nh = DIFF_TQ, DIFF_TK, DIFF_TQ // DIFF_TK
    hp = pl.program_id(0)
    i = pl.program_id(1)
    z_scrs = (z0_scr, z1_scr)
    zmax_scrs = (zmax0_scr, zmax1_scr)
    slopes = [slopes_ref[hp * DIFF_HPS + hh] for hh in range(DIFF_HPS)]

    row = lax.broadcasted_iota(jnp.int32, (LANES, tq), 0)
    for hh in range(DIFF_HPS):
        q = jnp.concatenate([q_ref[s, hh * LANES:(hh + 1) * LANES, :] for s in range(tq // LANES)], axis=1)
        qs = (q.astype(F32) * (DIFF_HEAD_DIM ** -0.5 * LOG2E)).astype(BF16)
        zero = jnp.zeros_like(qs)
        s_f32 = jnp.where(row < 2, slopes[hh], jnp.where(row < 4, slopes[hh] * tk, 0.0))
        s_hi = s_f32.astype(BF16)
        s_lo = (s_f32 - s_hi.astype(F32)).astype(BF16)
        srow = jnp.where(row % 2 == 0, s_hi, s_lo)
        qaug_scr[2 * hh, 0:LANES, :] = jnp.where(row < DIFF_HEAD_DIM, qs, zero)
        qaug_scr[2 * hh + 1, 0:LANES, :] = jnp.where(row >= DIFF_HEAD_DIM, qs, zero)
        qaug_scr[2 * hh, LANES:2 * LANES, :] = srow
        qaug_scr[2 * hh + 1, LANES:2 * LANES, :] = srow
    m_scr[...] = jnp.full_like(m_scr, NEG_INF)
    acc_scr[...] = jnp.zeros_like(acc_scr)

    def qk(hh, j):
        for s in range(nh):
            rows = pl.ds(pl.multiple_of(j * tq + s * tk, tk), tk)
            kc = jnp.concatenate([k_ref[rows, hh * LANES:(hh + 1) * LANES],
                                  e_ref[s * tk:(s + 1) * tk, :]], axis=1)
            for c in range(2):
                z = _dot(kc, qaug_scr[2 * hh + c])
                z_scrs[hh][c * nh + s] = z
                zmax_scrs[hh][c * nh + s] = jnp.max(z, axis=0, keepdims=True)

    def softmax_pv(hh, j, diagonal):
        vt = jnp.concatenate([v_ref[(tq // LANES) * j + s, hh * LANES:(hh + 1) * LANES, :]
                              for s in range(tq // LANES)], axis=1)
        vt = jnp.concatenate([vt, jnp.ones((DIFF_ACC_ROWS - DIFF_V_DIM, tq), BF16)], axis=0)
        koff = (jnp.zeros((1, tq), jnp.int32) + j * tq).astype(F32) * slopes[hh]
        kr = lax.broadcasted_iota(jnp.int32, (tk, tq), 0)
        qc = lax.broadcasted_iota(jnp.int32, (tk, tq), 1)
        for c in range(2):
            zs = [z_scrs[hh][c * nh + s] for s in range(nh)]
            if diagonal:
                zs = [jnp.where(kr + s * tk <= qc, z, NEG_INF) for s, z in enumerate(zs)]
                zmaxs = [jnp.max(z, axis=0, keepdims=True) for z in zs]
            else:
                zmaxs = [zmax_scrs[hh][c * nh + s] for s in range(nh)]
            m_old = m_scr[2 * hh + c]
            m_tile = functools.reduce(jnp.maximum, zmaxs)
            m_new = jnp.maximum(m_old, m_tile + koff)
            alpha = jnp.exp2(m_old - m_new)
            shift = m_new - koff
            p = jnp.concatenate([jnp.exp2(z - shift).astype(BF16) for z in zs], axis=0)
            acc_scr[2 * hh + c] = alpha * acc_scr[2 * hh + c] + _dot(vt, p)
            m_scr[2 * hh + c] = m_new

    qk(0, 0)

    def body(j, carry):
        qk(1, j)
        softmax_pv(0, j, False)
        qk(0, j + 1)
        softmax_pv(1, j, False)
        return carry

    lax.fori_loop(0, i, body, 0)
    qk(1, i)
    softmax_pv(0, i, True)
    softmax_pv(1, i, True)

    lp = lam_ref[...]
    lam = (jnp.exp(jnp.sum(lp[0:1, :] * lp[1:2, :], axis=1, keepdims=True))
           - jnp.exp(jnp.sum(lp[2:3, :] * lp[3:4, :], axis=1, keepdims=True)) + lam_init)
    for hh in range(DIFF_HPS):
        a0, a1 = acc_scr[2 * hh], acc_scr[2 * hh + 1]
        o0 = a0[0:DIFF_V_DIM, :] * (1.0 / a0[DIFF_V_DIM:DIFF_V_DIM + 1, :])
        o1 = a1[0:DIFF_V_DIM, :] * (1.0 / a1[DIFF_V_DIM:DIFF_V_DIM + 1, :])
        o = o0 - lam * o1
        y = o * lax.rsqrt(jnp.mean(o * o, axis=0, keepdims=True) + NORM_EPS)
        o_ref[:, hh * DIFF_V_DIM:(hh + 1) * DIFF_V_DIM] = ((y.T * g_ref[...]) * (1.0 - lam_init)).astype(BF16)


def _diff_attn(slopes, lam_params, subln, proj, proj_t, *, lam_init):
    tq, tk, hps = DIFF_TQ, DIFF_TK, DIFF_HPS
    w = hps * LANES
    k_col0 = N_GATE // w
    v_row0 = DIFF_WIDTH // w
    r = np.arange(tq)[:, None]
    c = np.arange(LANES)[None, :]
    e = jnp.asarray(np.where(c < 2, r % tk, np.where(c < 4, r // tk, 0)), dtype=BF16)
    return pl.pallas_call(
        functools.partial(_diff_body, lam_init=lam_init),
        grid=(DIFF_HEADS // hps, SEQ // tq),
        in_specs=[
            pl.BlockSpec(memory_space=pltpu.SMEM),
            pl.BlockSpec((4, DIFF_HEAD_DIM), lambda h, i: (0, 0)),
            pl.BlockSpec((tq // LANES, w, LANES), lambda h, i: (i, h, 0)),
            pl.BlockSpec((SEQ, w), lambda h, i: (0, k_col0 + h)),
            pl.BlockSpec((N_SLABS, w, LANES), lambda h, i: (0, v_row0 + h, 0)),
            pl.BlockSpec((tq, LANES), lambda h, i: (0, 0)),
            pl.BlockSpec((1, DIFF_V_DIM), lambda h, i: (0, 0)),
        ],
        out_specs=pl.BlockSpec((tq, hps * DIFF_V_DIM), lambda h, i: (i, h)),
        out_shape=jax.ShapeDtypeStruct((SEQ, DIFF_WIDTH), BF16),
        scratch_shapes=[
            pltpu.VMEM((2 * hps, 2 * LANES, tq), BF16),
            pltpu.VMEM((2 * (tq // tk), tk, tq), F32),
            pltpu.VMEM((2 * (tq // tk), tk, tq), F32),
            pltpu.VMEM((2 * (tq // tk), 1, tq), F32),
            pltpu.VMEM((2 * (tq // tk), 1, tq), F32),
            pltpu.VMEM((2 * hps, 1, tq), F32),
            pltpu.VMEM((2 * hps, DIFF_ACC_ROWS, tq), F32),
        ],
        compiler_params=_params("arbitrary", "arbitrary"),
        name="diff_attn",
    )(slopes, lam_params, proj_t, proj, proj_t, e, subln)


SWA_TQ = 512
SWA_BLOCKS = SWA_TQ // WINDOW


def _swa_bias(slope, off):
    kb = lax.broadcasted_iota(jnp.int32, (2 * WINDOW, WINDOW), 0)
    qa = lax.broadcasted_iota(jnp.int32, (2 * WINDOW, WINDOW), 1)
    dist = off + qa - kb
    return jnp.where((dist >= 0) & (dist < WINDOW), -slope * dist.astype(F32), NEG_INF)


def _swa_body(slopes_ref, sinks_ref, q_ref, k_ref, v_ref, o_ref, qaug_scr, bias_scr):
    i = pl.program_id(0)

    @pl.when(i == 0)
    def _():
        qaug_scr[...] = jnp.zeros_like(qaug_scr)
        for hd in range(SWA_Q_HEADS):
            bias_scr[hd] = _swa_bias(slopes_ref[hd], WINDOW)

    def block(t, slab0, bias_of):
        kwin = k_ref[pl.ds(pl.multiple_of(slab0 * WINDOW, WINDOW), 2 * WINDOW), :]
        vt = jnp.concatenate([v_ref[slab0], v_ref[slab0 + 1]], axis=1)
        ones = jnp.ones((16, 2 * WINDOW), BF16)
        qs = q_ref[t]
        zs = []
        for kh in range(SWA_KV_HEADS):
            r0 = kh * SWA_HEAD_DIM
            for g in range(SWA_GROUP):
                hd = kh * SWA_GROUP + g
                qh = qs[hd * SWA_HEAD_DIM:(hd + 1) * SWA_HEAD_DIM, :].astype(F32) * (SWA_HEAD_DIM ** -0.5 * LOG2E)
                qaug_scr[kh, r0:r0 + SWA_HEAD_DIM, g * LANES:(g + 1) * LANES] = qh.astype(BF16)
            zs.append(_dot(kwin, qaug_scr[kh]))
        outs = []
        for kh in range(SWA_KV_HEADS):
            r0 = kh * SWA_HEAD_DIM
            z = zs[kh]
            vt_kh = jnp.concatenate([vt[r0:r0 + SWA_HEAD_DIM, :], ones], axis=0)
            row = []
            for g in range(SWA_GROUP):
                hd = kh * SWA_GROUP + g
                sink = sinks_ref[hd]
                sc = z[:, g * LANES:(g + 1) * LANES] + bias_of(hd)
                m = jnp.maximum(jnp.max(sc, axis=0, keepdims=True), sink)
                p = jnp.exp2(sc - m).astype(BF16)
                ot = _dot(vt_kh, p)
                denom = ot[SWA_HEAD_DIM:SWA_HEAD_DIM + 1, :] + jnp.exp2(sink - m)
                row.append(ot[0:SWA_HEAD_DIM, :] * (1.0 / denom))
            outs.append(row)
        r = pl.multiple_of(t * WINDOW, WINDOW)
        for pr in range(SWA_KV_HEADS // 2):
            for g in range(SWA_GROUP):
                x = jnp.concatenate([outs[2 * pr][g], outs[2 * pr + 1][g]], axis=0)
                c0 = (pr * SWA_GROUP + g) * LANES
                o_ref[pl.ds(r, WINDOW), c0:c0 + LANES] = x.T.astype(BF16)

    @pl.when(i == 0)
    def _():
        block(0, 0, lambda hd: _swa_bias(slopes_ref[hd], 0))

    def body(t, carry):
        block(t, i * SWA_BLOCKS + t - 1, lambda hd: bias_scr[hd])
        return carry

    lax.fori_loop(jnp.where(i == 0, 1, 0), SWA_BLOCKS, body, 0)


def _swa_attn(slopes, sinks, proj, proj_t):
    q_row_blk = (2 * DIFF_WIDTH) // SWA_WIDTH
    k_col_blk = (N_GATE + DIFF_WIDTH) // SWA_KV_WIDTH
    v_row_blk = (2 * DIFF_WIDTH + SWA_WIDTH) // SWA_KV_WIDTH + 1
    return pl.pallas_call(
        _swa_body,
        grid=(SEQ // SWA_TQ,),
        in_specs=[
            pl.BlockSpec(memory_space=pltpu.SMEM),
            pl.BlockSpec(memory_space=pltpu.SMEM),
            pl.BlockSpec((SWA_BLOCKS, SWA_WIDTH, LANES), lambda i: (i, q_row_blk, 0)),
            pl.BlockSpec((SEQ, SWA_KV_WIDTH), lambda i: (0, k_col_blk)),
            pl.BlockSpec((N_SLABS, SWA_KV_WIDTH, LANES), lambda i: (0, v_row_blk, 0)),
        ],
        out_specs=pl.BlockSpec((SWA_TQ, SWA_WIDTH), lambda i: (i, 0)),
        out_shape=jax.ShapeDtypeStruct((SEQ, SWA_WIDTH), BF16),
        scratch_shapes=[pltpu.VMEM((SWA_KV_HEADS, SWA_KV_WIDTH, SWA_GROUP * LANES), BF16),
                        pltpu.VMEM((SWA_Q_HEADS, 2 * WINDOW, WINDOW), F32)],
        compiler_params=_params("arbitrary"),
        name="swa_attn",
    )(slopes, sinks, proj_t, proj, proj_t)


def _mix_body(a_ref, b_ref, ga_ref, gb_ref, wa_ref, wb_ref, wo_ref, x_ref, gpost_ref, o_ref):
    j = pl.program_id(1)

    @pl.when(j == 0)
    def _():
        o_ref[...] = jnp.zeros_like(o_ref)

    ya = _dot(a_ref[...], wa_ref[...])
    yb = _dot(b_ref[...], wb_ref[...])
    mixed = ga_ref[...].astype(F32) * ya + gb_ref[...].astype(F32) * yb
    o_ref[...] += _dot(mixed.astype(BF16), wo_ref[...])

    @pl.when(j == pl.num_programs(1) - 1)
    def _():
        o_ref[...] = x_ref[...] + _rms(o_ref[...]) * gpost_ref[...]


def _mix_out(a, b, proj, wa, wb, wo, x, gpost, *, tm=512, tn=1024):
    s, d = x.shape
    nj = d // tn
    return pl.pallas_call(
        _mix_body,
        grid=(s // tm, nj),
        in_specs=[
            pl.BlockSpec((tm, DIFF_WIDTH), lambda i, j: (i, 0)),
            pl.BlockSpec((tm, SWA_WIDTH), lambda i, j: (i, 0)),
            pl.BlockSpec((tm, tn), lambda i, j: (i, j)),
            pl.BlockSpec((tm, tn), lambda i, j: (i, nj + j)),
            pl.BlockSpec((DIFF_WIDTH, tn), lambda i, j: (0, j)),
            pl.BlockSpec((SWA_WIDTH, tn), lambda i, j: (0, j)),
            pl.BlockSpec((tn, d), lambda i, j: (j, 0)),
            pl.BlockSpec((tm, d), lambda i, j: (i, 0)),
            pl.BlockSpec((1, d), lambda i, j: (0, 0)),
        ],
        out_specs=pl.BlockSpec((tm, d), lambda i, j: (i, 0)),
        out_shape=jax.ShapeDtypeStruct((s, d), F32),
        compiler_params=_params("arbitrary", "arbitrary"),
        name="mix_out",
    )(a, b, proj, proj, wa, wb, wo, x, gpost)


def _alibi_slopes(n):
    return jnp.asarray(2.0 ** (-8.0 * np.arange(1, n + 1) / n), dtype=F32)


def kernel(x, ffn1_norm_pre, ffn1_w_in, ffn1_w_out, ffn1_norm_post, mix_norm_pre, w_in, diff_lambda, diff_subln, swa_sinks, w_branch_diff, w_branch_swa, w_out, mix_norm_post, ffn2_norm_pre, ffn2_w_in, ffn2_w_out, ffn2_norm_post):
    depth = ffn1_w_in.shape[0]
    xs = x.reshape(SEQ, D_MODEL)
    for l in range(depth):
        lam_init = 0.8 - 0.6 * math.exp(-0.3 * l)
        w = w_in[l].astype(BF16)
        wbs = (w_branch_swa[l].reshape(2, 2, SWA_GROUP, SWA_HEAD_DIM, D_MODEL)
               .transpose(0, 2, 1, 3, 4).reshape(SWA_WIDTH, D_MODEL).astype(BF16))
        wbd = w_branch_diff[l].astype(BF16)
        wo = w_out[l].astype(BF16)
        row = lambda v: v.reshape(1, -1)

        x1, h2 = _ffn(xs, row(ffn1_norm_pre[l]), ffn1_w_in[l].astype(BF16), ffn1_w_out[l].astype(BF16),
                      row(ffn1_norm_post[l]), row(mix_norm_pre[l]))
        proj = _proj(h2, w)
        proj_t = _proj_t(h2, w)
        a = _diff_attn(_alibi_slopes(DIFF_HEADS) * LOG2E, diff_lambda[l], row(diff_subln[l]), proj, proj_t,
                       lam_init=lam_init)
        b = _swa_attn(_alibi_slopes(SWA_Q_HEADS) * LOG2E, swa_sinks[l] * LOG2E, proj, proj_t)
        x2 = _mix_out(a, b, proj, wbd, wbs, wo, x1, row(mix_norm_post[l]))
        xs = _ffn(x2, row(ffn2_norm_pre[l]), ffn2_w_in[l].astype(BF16), ffn2_w_out[l].astype(BF16),
                  row(ffn2_norm_post[l]))
    return xs.reshape(x.shape)
```

```python
import functools
import math

import numpy as np
import jax
import jax.numpy as jnp
from jax import lax
from jax.experimental import pallas as pl
from jax.experimental.pallas import tpu as pltpu

D_MODEL = 2048
SEQ = 8192
DIFF_HEADS = 8
DIFF_HEAD_DIM = 64
DIFF_V_DIM = 128
DIFF_WIDTH = 1024
SWA_Q_HEADS = 16
SWA_KV_HEADS = 4
SWA_GROUP = 4
SWA_HEAD_DIM = 64
SWA_WIDTH = 1024
SWA_KV_WIDTH = 256
WINDOW = 128
D_FF = 5504
NORM_EPS = 1e-6
NEG_INF = -1e30

LANES = 128
MXU_DIM = 256
VMEM_LIMIT_BYTES = 60 * 1024 * 1024

N_GATE = 2 * D_MODEL
PROJ_COLS = N_GATE + DIFF_WIDTH + 2 * SWA_KV_WIDTH
PROJT_ROWS = DIFF_WIDTH + DIFF_WIDTH + SWA_WIDTH + 2 * SWA_KV_WIDTH
N_SLABS = SEQ // LANES

BF16 = jnp.bfloat16
F32 = jnp.float32


def _dot(a, b):
    return jnp.dot(a, b, preferred_element_type=F32)


def _rms(x):
    return x * lax.rsqrt(jnp.mean(x * x, axis=-1, keepdims=True) + NORM_EPS)


def _params(*sem, flags=None):
    return pltpu.CompilerParams(dimension_semantics=sem, vmem_limit_bytes=VMEM_LIMIT_BYTES, flags=flags)


FFN_ROW_CHUNK = 512


def _ffn_body(*refs, emit_next, overlap):
    if emit_next:
        (x_ref, gpre_ref, wg_ref, wu_ref, wo_ref, gpost_ref, gnext_ref,
         o_ref, hn_ref, h_scr) = refs
    else:
        (x_ref, gpre_ref, wg_ref, wu_ref, wo_ref, gpost_ref,
         o_ref, h_scr) = refs
    j = pl.program_id(1)
    last = pl.num_programs(1) - 1

    @pl.when(j == 0)
    def _():
        h_scr[...] = (_rms(x_ref[...]) * gpre_ref[...]).astype(BF16)
        o_ref[...] = jnp.zeros_like(o_ref)

    for r in range(0, h_scr.shape[0], FFN_ROW_CHUNK):
        rows = slice(r, r + FFN_ROW_CHUNK)
        h = h_scr[rows, :]
        g = _dot(h, wg_ref[...])
        u = _dot(h, wu_ref[...])
        act = jax.nn.silu(g) * u
        if overlap:
            col = lax.broadcasted_iota(jnp.int32, act.shape, 1)
            act = jnp.where((j == last) & (col < overlap), 0.0, act)
        o_ref[rows, :] += _dot(act.astype(BF16), wo_ref[...])

    @pl.when(j == last)
    def _():
        x1 = x_ref[...] + 0.5 * (_rms(o_ref[...]) * gpost_ref[...])
        o_ref[...] = x1
        if emit_next:
            hn_ref[...] = (_rms(x1) * gnext_ref[...]).astype(BF16)


def _ffn(x, gpre, w_in, w_out, gpost, gnext=None, *, tf=512):
    emit_next = gnext is not None
    tm = 1024
    x_mode = dict(pipeline_mode=pl.Buffered(1)) if emit_next else {}
    s, d = x.shape
    d_ff = w_out.shape[0]
    nj = pl.cdiv(d_ff, tf)
    overlap = nj * tf - d_ff
    assert tf % LANES == 0 and d_ff % LANES == 0
    start = lambda j, base=0: (jnp.minimum(j * (tf // LANES), (d_ff - tf) // LANES) + base // LANES) * LANES
    row = lambda i, j: (i, 0)
    fixed = lambda i, j: (0, 0)
    in_specs = [
        pl.BlockSpec((tm, d), row, **x_mode),
        pl.BlockSpec((1, d), fixed),
        pl.BlockSpec((pl.Element(d), pl.Element(tf)), lambda i, j: (0, start(j))),
        pl.BlockSpec((pl.Element(d), pl.Element(tf)), lambda i, j: (0, start(j, d_ff))),
        pl.BlockSpec((pl.Element(tf), pl.Element(d)), lambda i, j: (start(j), 0)),
        pl.BlockSpec((1, d), fixed),
    ]
    args = [x, gpre, w_in, w_in, w_out, gpost]
    out_shape = [jax.ShapeDtypeStruct((s, d), F32)]
    out_specs = [pl.BlockSpec((tm, d), row)]
    if emit_next:
        in_specs.append(pl.BlockSpec((1, d), fixed))
        args.append(gnext)
        out_shape.append(jax.ShapeDtypeStruct((s, d), BF16))
        out_specs.append(pl.BlockSpec((tm, d), row))
    res = pl.pallas_call(
        functools.partial(_ffn_body, emit_next=emit_next, overlap=overlap),
        grid=(s // tm, nj),
        in_specs=in_specs,
        out_specs=out_specs,
        out_shape=out_shape,
        scratch_shapes=[pltpu.VMEM((tm, d), BF16)],
        compiler_params=_params("arbitrary", "arbitrary"),
        name="ffn_next" if emit_next else "ffn",
    )(*args)
    return res if emit_next else res[0]


PROJ_TN = 2 * MXU_DIM
assert 2 * SWA_KV_WIDTH == PROJ_TN
_O_DQ, _O_DK, _O_DV, _O_SQ = (k * DIFF_WIDTH // PROJ_TN for k in range(4))
_O_SKV = _O_SQ + SWA_WIDTH // PROJ_TN
_O_G = _O_SKV + 1
_N_G, _N_D, _N_S = N_GATE // PROJ_TN, DIFF_WIDTH // PROJ_TN, SWA_WIDTH // PROJ_TN


def _proj_src_tile(t):
    return jnp.where(t < _N_G, _O_G + t, jnp.where(t < _N_G + _N_D, _O_DK + (t - _N_G), _O_SKV))


def _proj_t_src_tile(t):
    return jnp.where(t < _N_D, _O_DQ + t,
                     jnp.where(t < 2 * _N_D, _O_DV + (t - _N_D),
                               jnp.where(t < 2 * _N_D + _N_S, _O_SQ + (t - 2 * _N_D), _O_SKV)))


PROJ_ROW_CHUNK = 256


def _proj_body(h_ref, w_ref, o_ref, w_scr):
    def run(gated):
        w_scr[...] = w_ref[...].astype(BF16)
        for r in range(0, h_ref.shape[0], PROJ_ROW_CHUNK):
            rows = slice(r, r + PROJ_ROW_CHUNK)
            res = _dot(h_ref[rows, :], w_scr[...])
            if gated:
                res = 0.5 * jnp.tanh(0.5 * res) + 0.5
            o_ref[rows, :] = res.astype(BF16)

    gated = pl.program_id(1) < _N_G
    pl.when(gated)(lambda: run(True))
    pl.when(jnp.logical_not(gated))(lambda: run(False))


def _proj(h, w, *, tm=2048):
    s, d = h.shape
    tn = PROJ_TN
    return pl.pallas_call(
        _proj_body,
        grid=(s // tm, PROJ_COLS // tn),
        in_specs=[pl.BlockSpec((tm, d), lambda i, t: (i, 0)),
                  pl.BlockSpec((d, tn), lambda i, t: (0, _proj_src_tile(t)))],
        out_specs=pl.BlockSpec((tm, tn), lambda i, t: (i, t)),
        out_shape=jax.ShapeDtypeStruct((s, PROJ_COLS), BF16),
        scratch_shapes=[pltpu.VMEM((d, tn), BF16)],
        compiler_params=_params("arbitrary", "arbitrary"),
        name="proj",
    )(h, w)


def _proj_t_body(h_ref, w_ref, o_ref, w_scr):
    w_scr[...] = w_ref[...].astype(BF16)
    for r in range(0, h_ref.shape[0], PROJ_ROW_CHUNK):
        res = _dot(h_ref[r:r + PROJ_ROW_CHUNK, :], w_scr[...])
        for k in range(PROJ_ROW_CHUNK // LANES):
            o_ref[r // LANES + k] = res[k * LANES:(k + 1) * LANES, :].T.astype(BF16)


def _proj_t(h, w, *, tm=2048):
    s, d = h.shape
    tn = PROJ_TN
    return pl.pallas_call(
        _proj_t_body,
        grid=(s // tm, PROJT_ROWS // tn),
        in_specs=[pl.BlockSpec((tm, d), lambda i, t: (i, 0)),
                  pl.BlockSpec((d, tn), lambda i, t: (0, _proj_t_src_tile(t)))],
        out_specs=pl.BlockSpec((tm // LANES, tn, LANES), lambda i, t: (i, t, 0)),
        out_shape=jax.ShapeDtypeStruct((s // LANES, PROJT_ROWS, LANES), BF16),
        scratch_shapes=[pltpu.VMEM((d, tn), BF16)],
        compiler_params=_params("arbitrary", "arbitrary"),
        name="proj_t",
    )(h, w)


DIFF_TQ = 512
DIFF_TK = 256
DIFF_ACC_ROWS = DIFF_V_DIM + 16
LOG2E = math.log2(math.e)


DIFF_HPS = 4


def _diff_body(slopes_ref, lam_ref, q_ref, k_ref, v_ref, e_ref, g_ref, o_ref, qaug_scr, *scr, lam_init):
    tq, tk, nh = DIFF_TQ, DIFF_TK, DIFF_TQ // DIFF_TK
    hp = pl.program_id(0)
    i = pl.program_id(1)
    z_scrs, zmax_scrs = scr[:DIFF_HPS], scr[DIFF_HPS:2 * DIFF_HPS]
    m_scr, acc_scr = scr[2 * DIFF_HPS:]
    slopes = [slopes_ref[hp * DIFF_HPS + hh] for hh in range(DIFF_HPS)]

    row = lax.broadcasted_iota(jnp.int32, (LANES, tq), 0)
    for hh in range(DIFF_HPS):
        q = jnp.concatenate([q_ref[s, hh * LANES:(hh + 1) * LANES, :] for s in range(tq // LANES)], axis=1)
        qs = (q.astype(F32) * (DIFF_HEAD_DIM ** -0.5 * LOG2E)).astype(BF16)
        zero = jnp.zeros_like(qs)
        s_f32 = jnp.where(row < 2, slopes[hh], jnp.where(row < 4, slopes[hh] * tk, 0.0))
        s_hi = s_f32.astype(BF16)
        s_lo = (s_f32 - s_hi.astype(F32)).astype(BF16)
        srow = jnp.where(row % 2 == 0, s_hi, s_lo)
        qaug_scr[2 * hh, 0:LANES, :] = jnp.where(row < DIFF_HEAD_DIM, qs, zero)
        qaug_scr[2 * hh + 1, 0:LANES, :] = jnp.where(row >= DIFF_HEAD_DIM, qs, zero)
        qaug_scr[2 * hh, LANES:2 * LANES, :] = srow
        qaug_scr[2 * hh + 1, LANES:2 * LANES, :] = srow
    m_scr[...] = jnp.full_like(m_scr, NEG_INF)
    acc_scr[...] = jnp.zeros_like(acc_scr)

    def qk(hh, j):
        for s in range(nh):
            rows = pl.ds(pl.multiple_of(j * tq + s * tk, tk), tk)
            kc = jnp.concatenate([k_ref[rows, hh * LANES:(hh + 1) * LANES],
                                  e_ref[s * tk:(s + 1) * tk, :]], axis=1)
            for c in range(2):
                z = _dot(kc, qaug_scr[2 * hh + c])
                z_scrs[hh][c * nh + s] = z
                zmax_scrs[hh][c * nh + s] = jnp.max(z, axis=0, keepdims=True)

    def softmax_pv(hh, j, diagonal):
        vt = jnp.concatenate([v_ref[(tq // LANES) * j + s, hh * LANES:(hh + 1) * LANES, :]
                              for s in range(tq // LANES)], axis=1)
        vt = jnp.concatenate([vt, jnp.ones((DIFF_ACC_ROWS - DIFF_V_DIM, tq), BF16)], axis=0)
        koff = (jnp.zeros((1, tq), jnp.int32) + j * tq).astype(F32) * slopes[hh]
        kr = lax.broadcasted_iota(jnp.int32, (tk, tq), 0)
        qc = lax.broadcasted_iota(jnp.int32, (tk, tq), 1)
        for c in range(2):
            zs = [z_scrs[hh][c * nh + s] for s in range(nh)]
            if diagonal:
                zs = [jnp.where(kr + s * tk <= qc, z, NEG_INF) for s, z in enumerate(zs)]
                zmaxs = [jnp.max(z, axis=0, keepdims=True) for z in zs]
            else:
                zmaxs = [zmax_scrs[hh][c * nh + s] for s in range(nh)]
            m_old = m_scr[2 * hh + c]
            m_tile = functools.reduce(jnp.maximum, zmaxs)
            m_new = jnp.maximum(m_old, m_tile + koff)
            alpha = jnp.exp2(m_old - m_new)
            shift = m_new - koff
            p = jnp.concatenate([jnp.exp2(z - shift).astype(BF16) for z in zs], axis=0)
            acc_scr[2 * hh + c] = alpha * acc_scr[2 * hh + c] + _dot(vt, p)
            m_scr[2 * hh + c] = m_new

    group_a = range(0, DIFF_HPS // 2)
    group_b = range(DIFF_HPS // 2, DIFF_HPS)
    for hh in group_a:
        qk(hh, 0)

    def body(j, carry):
        for hh in group_b:
            qk(hh, j)
        for hh in group_a:
            softmax_pv(hh, j, False)
        for hh in group_a:
            qk(hh, j + 1)
        for hh in group_b:
            softmax_pv(hh, j, False)
        return carry

    lax.fori_loop(0, i, body, 0)
    for hh in group_b:
        qk(hh, i)
    for hh in range(DIFF_HPS):
        softmax_pv(hh, i, True)

    lp = lam_ref[...]
    lam = (jnp.exp(jnp.sum(lp[0:1, :] * lp[1:2, :], axis=1, keepdims=True))
           - jnp.exp(jnp.sum(lp[2:3, :] * lp[3:4, :], axis=1, keepdims=True)) + lam_init)
    for hh in range(DIFF_HPS):
        a0, a1 = acc_scr[2 * hh], acc_scr[2 * hh + 1]
        o0 = a0[0:DIFF_V_DIM, :] * (1.0 / a0[DIFF_V_DIM:DIFF_V_DIM + 1, :])
        o1 = a1[0:DIFF_V_DIM, :] * (1.0 / a1[DIFF_V_DIM:DIFF_V_DIM + 1, :])
        o = o0 - lam * o1
        y = o * lax.rsqrt(jnp.mean(o * o, axis=0, keepdims=True) + NORM_EPS)
        o_ref[:, hh * DIFF_V_DIM:(hh + 1) * DIFF_V_DIM] = ((y.T * g_ref[...]) * (1.0 - lam_init)).astype(BF16)


def _diff_attn(slopes, lam_params, subln, proj, proj_t, *, lam_init):
    tq, tk, hps = DIFF_TQ, DIFF_TK, DIFF_HPS
    w = hps * LANES
    k_col0 = N_GATE // w
    v_row0 = DIFF_WIDTH // w
    r = np.arange(tq)[:, None]
    c = np.arange(LANES)[None, :]
    e = jnp.asarray(np.where(c < 2, r % tk, np.where(c < 4, r // tk, 0)), dtype=BF16)
    return pl.pallas_call(
        functools.partial(_diff_body, lam_init=lam_init),
        grid=(DIFF_HEADS // hps, SEQ // tq),
        in_specs=[
            pl.BlockSpec(memory_space=pltpu.SMEM),
            pl.BlockSpec((4, DIFF_HEAD_DIM), lambda h, i: (0, 0)),
            pl.BlockSpec((tq // LANES, w, LANES), lambda h, i: (i, h, 0)),
            pl.BlockSpec((SEQ, w), lambda h, i: (0, k_col0 + h)),
            pl.BlockSpec((N_SLABS, w, LANES), lambda h, i: (0, v_row0 + h, 0)),
            pl.BlockSpec((tq, LANES), lambda h, i: (0, 0)),
            pl.BlockSpec((1, DIFF_V_DIM), lambda h, i: (0, 0)),
        ],
        out_specs=pl.BlockSpec((tq, hps * DIFF_V_DIM), lambda h, i: (i, h)),
        out_shape=jax.ShapeDtypeStruct((SEQ, DIFF_WIDTH), BF16),
        scratch_shapes=[
            pltpu.VMEM((2 * hps, 2 * LANES, tq), BF16),
            *[pltpu.VMEM((2 * (tq // tk), tk, tq), F32)] * hps,
            *[pltpu.VMEM((2 * (tq // tk), 1, tq), F32)] * hps,
            pltpu.VMEM((2 * hps, 1, tq), F32),
            pltpu.VMEM((2 * hps, DIFF_ACC_ROWS, tq), F32),
        ],
        compiler_params=_params("arbitrary", "arbitrary"),
        name="diff_attn",
    )(slopes, lam_params, proj_t, proj, proj_t, e, subln)


SWA_TQ = 512
SWA_BLOCKS = SWA_TQ // WINDOW


def _swa_bias(slope, off):
    kb = lax.broadcasted_iota(jnp.int32, (2 * WINDOW, WINDOW), 0)
    qa = lax.broadcasted_iota(jnp.int32, (2 * WINDOW, WINDOW), 1)
    dist = off + qa - kb
    return jnp.where((dist >= 0) & (dist < WINDOW), -slope * dist.astype(F32), NEG_INF)


def _swa_body(slopes_ref, sinks_ref, q_ref, k_ref, v_ref, o_ref, qaug_scr, bias_scr):
    i = pl.program_id(0)

    @pl.when(i == 0)
    def _():
        qaug_scr[...] = jnp.zeros_like(qaug_scr)
        for hd in range(SWA_Q_HEADS):
            bias_scr[hd] = _swa_bias(slopes_ref[hd], WINDOW)

    def block(t, slab0, bias_of):
        kwin = k_ref[pl.ds(pl.multiple_of(slab0 * WINDOW, WINDOW), 2 * WINDOW), :]
        vt = jnp.concatenate([v_ref[slab0], v_ref[slab0 + 1]], axis=1)
        ones = jnp.ones((16, 2 * WINDOW), BF16)
        qs = q_ref[t]
        zs = []
        for kh in range(SWA_KV_HEADS):
            r0 = kh * SWA_HEAD_DIM
            for g in range(SWA_GROUP):
                hd = kh * SWA_GROUP + g
                qh = qs[hd * SWA_HEAD_DIM:(hd + 1) * SWA_HEAD_DIM, :].astype(F32) * (SWA_HEAD_DIM ** -0.5 * LOG2E)
                qaug_scr[kh, r0:r0 + SWA_HEAD_DIM, g * LANES:(g + 1) * LANES] = qh.astype(BF16)
            zs.append(_dot(kwin, qaug_scr[kh]))
        outs = []
        for kh in range(SWA_KV_HEADS):
            r0 = kh * SWA_HEAD_DIM
            z = zs[kh]
            vt_kh = jnp.concatenate([vt[r0:r0 + SWA_HEAD_DIM, :], ones], axis=0)
            row = []
            for g in range(SWA_GROUP):
                hd = kh * SWA_GROUP + g
                sink = sinks_ref[hd]
                sc = z[:, g * LANES:(g + 1) * LANES] + bias_of(hd)
                m = jnp.maximum(jnp.max(sc, axis=0, keepdims=True), sink)
                p = jnp.exp2(sc - m).astype(BF16)
                ot = _dot(vt_kh, p)
                denom = ot[SWA_HEAD_DIM:SWA_HEAD_DIM + 1, :] + jnp.exp2(sink - m)
                row.append(ot[0:SWA_HEAD_DIM, :] * (1.0 / denom))
            outs.append(row)
        r = pl.multiple_of(t * WINDOW, WINDOW)
        for pr in range(SWA_KV_HEADS // 2):
            for g in range(SWA_GROUP):
                x = jnp.concatenate([outs[2 * pr][g], outs[2 * pr + 1][g]], axis=0)
                c0 = (pr * SWA_GROUP + g) * LANES
                o_ref[pl.ds(r, WINDOW), c0:c0 + LANES] = x.T.astype(BF16)

    @pl.when(i == 0)
    def _():
        block(0, 0, lambda hd: _swa_bias(slopes_ref[hd], 0))

    def body(t, carry):
        block(t, i * SWA_BLOCKS + t - 1, lambda hd: bias_scr[hd])
        return carry

    lax.fori_loop(jnp.where(i == 0, 1, 0), SWA_BLOCKS, body, 0)


def _swa_attn(slopes, sinks, proj, proj_t):
    q_row_blk = (2 * DIFF_WIDTH) // SWA_WIDTH
    k_col_blk = (N_GATE + DIFF_WIDTH) // SWA_KV_WIDTH
    v_row_blk = (2 * DIFF_WIDTH + SWA_WIDTH) // SWA_KV_WIDTH + 1
    return pl.pallas_call(
        _swa_body,
        grid=(SEQ // SWA_TQ,),
        in_specs=[
            pl.BlockSpec(memory_space=pltpu.SMEM),
            pl.BlockSpec(memory_space=pltpu.SMEM),
            pl.BlockSpec((SWA_BLOCKS, SWA_WIDTH, LANES), lambda i: (i, q_row_blk, 0)),
            pl.BlockSpec((SEQ, SWA_KV_WIDTH), lambda i: (0, k_col_blk)),
            pl.BlockSpec((N_SLABS, SWA_KV_WIDTH, LANES), lambda i: (0, v_row_blk, 0)),
        ],
        out_specs=pl.BlockSpec((SWA_TQ, SWA_WIDTH), lambda i: (i, 0)),
        out_shape=jax.ShapeDtypeStruct((SEQ, SWA_WIDTH), BF16),
        scratch_shapes=[pltpu.VMEM((SWA_KV_HEADS, SWA_KV_WIDTH, SWA_GROUP * LANES), BF16),
                        pltpu.VMEM((SWA_Q_HEADS, 2 * WINDOW, WINDOW), F32)],
        compiler_params=_params("arbitrary"),
        name="swa_attn",
    )(slopes, sinks, proj_t, proj, proj_t)


def _mix_body(a_ref, b_ref, ga_ref, gb_ref, wa_ref, wb_ref, wo_ref, x_ref, gpost_ref, o_ref):
    j = pl.program_id(1)

    @pl.when(j == 0)
    def _():
        o_ref[...] = jnp.zeros_like(o_ref)

    ya = _dot(a_ref[...], wa_ref[...])
    yb = _dot(b_ref[...], wb_ref[...])
    mixed = ga_ref[...].astype(F32) * ya + gb_ref[...].astype(F32) * yb
    o_ref[...] += _dot(mixed.astype(BF16), wo_ref[...])

    @pl.when(j == pl.num_programs(1) - 1)
    def _():
        o_ref[...] = x_ref[...] + _rms(o_ref[...]) * gpost_ref[...]


def _mix_out(a, b, proj, wa, wb, wo, x, gpost, *, tm=512, tn=1024):
    s, d = x.shape
    nj = d // tn
    return pl.pallas_call(
        _mix_body,
        grid=(s // tm, nj),
        in_specs=[
            pl.BlockSpec((tm, DIFF_WIDTH), lambda i, j: (i, 0)),
            pl.BlockSpec((tm, SWA_WIDTH), lambda i, j: (i, 0)),
            pl.BlockSpec((tm, tn), lambda i, j: (i, j)),
            pl.BlockSpec((tm, tn), lambda i, j: (i, nj + j)),
            pl.BlockSpec((DIFF_WIDTH, tn), lambda i, j: (0, j)),
            pl.BlockSpec((SWA_WIDTH, tn), lambda i, j: (0, j)),
            pl.BlockSpec((tn, d), lambda i, j: (j, 0)),
            pl.BlockSpec((tm, d), lambda i, j: (i, 0)),
            pl.BlockSpec((1, d), lambda i, j: (0, 0)),
        ],
        out_specs=pl.BlockSpec((tm, d), lambda i, j: (i, 0)),
        out_shape=jax.ShapeDtypeStruct((s, d), F32),
        compiler_params=_params("arbitrary", "arbitrary"),
        name="mix_out",
    )(a, b, proj, proj, wa, wb, wo, x, gpost)


def _alibi_slopes(n):
    return jnp.asarray(2.0 ** (-8.0 * np.arange(1, n + 1) / n), dtype=F32)


def kernel(x, ffn1_norm_pre, ffn1_w_in, ffn1_w_out, ffn1_norm_post, mix_norm_pre, w_in, diff_lambda, diff_subln, swa_sinks, w_branch_diff, w_branch_swa, w_out, mix_norm_post, ffn2_norm_pre, ffn2_w_in, ffn2_w_out, ffn2_norm_post):
    depth = ffn1_w_in.shape[0]
    xs = x.reshape(SEQ, D_MODEL)
    for l in range(depth):
        lam_init = 0.8 - 0.6 * math.exp(-0.3 * l)
        w = w_in[l]
        wbs = (w_branch_swa[l].reshape(2, 2, SWA_GROUP, SWA_HEAD_DIM, D_MODEL)
               .transpose(0, 2, 1, 3, 4).reshape(SWA_WIDTH, D_MODEL).astype(BF16))
        wbd = w_branch_diff[l].astype(BF16)
        wo = w_out[l].astype(BF16)
        row = lambda v: v.reshape(1, -1)

        x1, h2 = _ffn(xs, row(ffn1_norm_pre[l]), ffn1_w_in[l].astype(BF16), ffn1_w_out[l].astype(BF16),
                      row(ffn1_norm_post[l]), row(mix_norm_pre[l]))
        proj = _proj(h2, w)
        proj_t = _proj_t(h2, w)
        a = _diff_attn(_alibi_slopes(DIFF_HEADS) * LOG2E, diff_lambda[l], row(diff_subln[l]), proj, proj_t,
                       lam_init=lam_init)
        b = _swa_attn(_alibi_slopes(SWA_Q_HEADS) * LOG2E, swa_sinks[l] * LOG2E, proj, proj_t)
        x2 = _mix_out(a, b, proj, wbd, wbs, wo, x1, row(mix_norm_post[l]))
        xs = _ffn(x2, row(ffn2_norm_pre[l]), ffn2_w_in[l].astype(BF16), ffn2_w_out[l].astype(BF16),
                  row(ffn2_norm_post[l]))
    return xs.reshape(x.shape)
```

```python
import functools
import math

import numpy as np
import jax
import jax.numpy as jnp
from jax import lax
from jax.experimental import pallas as pl
from jax.experimental.pallas import tpu as pltpu

D_MODEL = 2048
SEQ = 8192
DIFF_HEADS = 8
DIFF_HEAD_DIM = 64
DIFF_V_DIM = 128
DIFF_WIDTH = 1024
SWA_Q_HEADS = 16
SWA_KV_HEADS = 4
SWA_GROUP = 4
SWA_HEAD_DIM = 64
SWA_WIDTH = 1024
SWA_KV_WIDTH = 256
WINDOW = 128
D_FF = 5504
NORM_EPS = 1e-6
NEG_INF = -1e30

LANES = 128
MXU_DIM = 256
VMEM_LIMIT_BYTES = 60 * 1024 * 1024

N_GATE = 2 * D_MODEL
PROJ_COLS = N_GATE + DIFF_WIDTH + 2 * SWA_KV_WIDTH
PROJT_ROWS = DIFF_WIDTH + DIFF_WIDTH + SWA_WIDTH + 2 * SWA_KV_WIDTH
N_SLABS = SEQ // LANES

BF16 = jnp.bfloat16
F32 = jnp.float32


def _dot(a, b):
    return jnp.dot(a, b, preferred_element_type=F32)


def _rms(x):
    return x * lax.rsqrt(jnp.mean(x * x, axis=-1, keepdims=True) + NORM_EPS)


def _params(*sem, flags=None):
    return pltpu.CompilerParams(dimension_semantics=sem, vmem_limit_bytes=VMEM_LIMIT_BYTES, flags=flags)


FFN_ROW_CHUNK = 512


def _ffn_body(*refs, emit_next, overlap, n_convert):
    refs = list(refs)
    x_ref, gpre_ref, wg_ref, wu_ref, wo_ref, gpost_ref = refs[:6]
    del refs[:6]
    gnext_ref = refs.pop(0) if emit_next else None
    cvt_in = [refs.pop(0) for _ in range(n_convert)]
    o_ref = refs.pop(0)
    hn_ref = refs.pop(0) if emit_next else None
    cvt_out = [refs.pop(0) for _ in range(n_convert)]
    (h_scr,) = refs
    j = pl.program_id(1)
    last = pl.num_programs(1) - 1

    @pl.when(j == 0)
    def _():
        h_scr[...] = (_rms(x_ref[...]) * gpre_ref[...]).astype(BF16)
        o_ref[...] = jnp.zeros_like(o_ref)

    for src, dst in zip(cvt_in, cvt_out):
        dst[...] = src[...].astype(BF16)

    for r in range(0, h_scr.shape[0], FFN_ROW_CHUNK):
        rows = slice(r, r + FFN_ROW_CHUNK)
        h = h_scr[rows, :]
        g = _dot(h, wg_ref[...])
        u = _dot(h, wu_ref[...])
        act = jax.nn.silu(g) * u
        if overlap:
            col = lax.broadcasted_iota(jnp.int32, act.shape, 1)
            act = jnp.where((j == last) & (col < overlap), 0.0, act)
        o_ref[rows, :] += _dot(act.astype(BF16), wo_ref[...])

    @pl.when(j == last)
    def _():
        x1 = x_ref[...] + 0.5 * (_rms(o_ref[...]) * gpost_ref[...])
        o_ref[...] = x1
        if emit_next:
            hn_ref[...] = (_rms(x1) * gnext_ref[...]).astype(BF16)


def _ffn(x, gpre, w_in, w_out, gpost, gnext=None, convert=(), *, tm=1024, tf=512):
    emit_next = gnext is not None
    s, d = x.shape
    d_ff = w_out.shape[0]
    nj = pl.cdiv(d_ff, tf)
    overlap = nj * tf - d_ff
    assert tf % LANES == 0 and d_ff % LANES == 0
    start = lambda j, base=0: (jnp.minimum(j * (tf // LANES), (d_ff - tf) // LANES) + base // LANES) * LANES
    row = lambda i, j: (i, 0)
    fixed = lambda i, j: (0, 0)
    in_specs = [
        pl.BlockSpec((tm, d), row),
        pl.BlockSpec((1, d), fixed),
        pl.BlockSpec((pl.Element(d), pl.Element(tf)), lambda i, j: (0, start(j))),
        pl.BlockSpec((pl.Element(d), pl.Element(tf)), lambda i, j: (0, start(j, d_ff))),
        pl.BlockSpec((pl.Element(tf), pl.Element(d)), lambda i, j: (start(j), 0)),
        pl.BlockSpec((1, d), fixed),
    ]
    args = [x, gpre, w_in, w_in, w_out, gpost]
    out_shape = [jax.ShapeDtypeStruct((s, d), F32)]
    out_specs = [pl.BlockSpec((tm, d), row)]
    if emit_next:
        in_specs.append(pl.BlockSpec((1, d), fixed))
        args.append(gnext)
        out_shape.append(jax.ShapeDtypeStruct((s, d), BF16))
        out_specs.append(pl.BlockSpec((tm, d), row))
    n_steps = (s // tm) * nj
    for m in convert:
        rb = 16 * pl.cdiv(m.shape[0], 16 * n_steps)
        assert m.shape[0] % rb == 0
        blk = pl.BlockSpec((rb, m.shape[1]), lambda i, j, nb=m.shape[0] // rb: (jnp.minimum(i * nj + j, nb - 1), 0))
        in_specs.append(blk)
        args.append(m)
        out_shape.append(jax.ShapeDtypeStruct(m.shape, BF16))
        out_specs.append(blk)
    res = pl.pallas_call(
        functools.partial(_ffn_body, emit_next=emit_next, overlap=overlap, n_convert=len(convert)),
        grid=(s // tm, nj),
        in_specs=in_specs,
        out_specs=out_specs,
        out_shape=out_shape,
        scratch_shapes=[pltpu.VMEM((tm, d), BF16)],
        compiler_params=_params("arbitrary", "arbitrary"),
        name="ffn_next" if emit_next else "ffn",
    )(*args)
    return res


PROJ_TN = 2 * MXU_DIM
assert 2 * SWA_KV_WIDTH == PROJ_TN
_O_DQ, _O_DK, _O_DV, _O_SQ = (k * DIFF_WIDTH // PROJ_TN for k in range(4))
_O_SKV = _O_SQ + SWA_WIDTH // PROJ_TN
_O_G = _O_SKV + 1
_N_G, _N_D, _N_S = N_GATE // PROJ_TN, DIFF_WIDTH // PROJ_TN, SWA_WIDTH // PROJ_TN


def _proj_src_tile(t):
    return jnp.where(t < _N_G, _O_G + t, jnp.where(t < _N_G + _N_D, _O_DK + (t - _N_G), _O_SKV))


def _proj_t_src_tile(t):
    return jnp.where(t < _N_D, _O_DQ + t,
                     jnp.where(t < 2 * _N_D, _O_DV + (t - _N_D),
                               jnp.where(t < 2 * _N_D + _N_S, _O_SQ + (t - 2 * _N_D), _O_SKV)))


PROJ_ROW_CHUNK = 256


def _proj_body(h_ref, w_ref, o_ref, w_scr):
    def run(gated):
        w_scr[...] = w_ref[...].astype(BF16)
        for r in range(0, h_ref.shape[0], PROJ_ROW_CHUNK):
            rows = slice(r, r + PROJ_ROW_CHUNK)
            res = _dot(h_ref[rows, :], w_scr[...])
            if gated:
                res = 0.5 * jnp.tanh(0.5 * res) + 0.5
            o_ref[rows, :] = res.astype(BF16)

    gated = pl.program_id(1) < _N_G
    pl.when(gated)(lambda: run(True))
    pl.when(jnp.logical_not(gated))(lambda: run(False))


def _proj(h, w, *, tm=2048):
    s, d = h.shape
    tn = PROJ_TN
    return pl.pallas_call(
        _proj_body,
        grid=(s // tm, PROJ_COLS // tn),
        in_specs=[pl.BlockSpec((tm, d), lambda i, t: (i, 0)),
                  pl.BlockSpec((d, tn), lambda i, t: (0, _proj_src_tile(t)))],
        out_specs=pl.BlockSpec((tm, tn), lambda i, t: (i, t)),
        out_shape=jax.ShapeDtypeStruct((s, PROJ_COLS), BF16),
        scratch_shapes=[pltpu.VMEM((d, tn), BF16)],
        compiler_params=_params("arbitrary", "arbitrary"),
        name="proj",
    )(h, w)


def _proj_t_body(h_ref, w_ref, o_ref, w_scr):
    w_scr[...] = w_ref[...].astype(BF16)
    for r in range(0, h_ref.shape[0], PROJ_ROW_CHUNK):
        res = _dot(h_ref[r:r + PROJ_ROW_CHUNK, :], w_scr[...])
        for k in range(PROJ_ROW_CHUNK // LANES):
            o_ref[r // LANES + k] = res[k * LANES:(k + 1) * LANES, :].T.astype(BF16)


def _proj_t(h, w, *, tm=2048):
    s, d = h.shape
    tn = PROJ_TN
    return pl.pallas_call(
        _proj_t_body,
        grid=(s // tm, PROJT_ROWS // tn),
        in_specs=[pl.BlockSpec((tm, d), lambda i, t: (i, 0)),
                  pl.BlockSpec((d, tn), lambda i, t: (0, _proj_t_src_tile(t)))],
        out_specs=pl.BlockSpec((tm // LANES, tn, LANES), lambda i, t: (i, t, 0)),
        out_shape=jax.ShapeDtypeStruct((s // LANES, PROJT_ROWS, LANES), BF16),
        scratch_shapes=[pltpu.VMEM((d, tn), BF16)],
        compiler_params=_params("arbitrary", "arbitrary"),
        name="proj_t",
    )(h, w)


DIFF_TQ = 512
DIFF_TK = 256
DIFF_ACC_ROWS = DIFF_V_DIM + 16
DIFF_HPS = 4
LOG2E = math.log2(math.e)


def _diff_body(slopes_ref, lam_ref, q_ref, k_ref, v_ref, e_ref, g_ref, o_ref, qaug_scr, *scr, lam_init):
    tq, tk, nh = DIFF_TQ, DIFF_TK, DIFF_TQ // DIFF_TK
    hp = pl.program_id(0)
    i = pl.program_id(1)
    z_scrs, zmax_scrs = scr[:DIFF_HPS], scr[DIFF_HPS:2 * DIFF_HPS]
    m_scr, acc_scr = scr[2 * DIFF_HPS:]
    slopes = [slopes_ref[hp * DIFF_HPS + hh] for hh in range(DIFF_HPS)]

    row = lax.broadcasted_iota(jnp.int32, (LANES, tq), 0)
    for hh in range(DIFF_HPS):
        q = jnp.concatenate([q_ref[s, hh * LANES:(hh + 1) * LANES, :] for s in range(tq // LANES)], axis=1)
        qs = (q.astype(F32) * (DIFF_HEAD_DIM ** -0.5 * LOG2E)).astype(BF16)
        zero = jnp.zeros_like(qs)
        s_f32 = jnp.where(row < 2, slopes[hh], jnp.where(row < 4, slopes[hh] * tk, 0.0))
        s_hi = s_f32.astype(BF16)
        s_lo = (s_f32 - s_hi.astype(F32)).astype(BF16)
        srow = jnp.where(row % 2 == 0, s_hi, s_lo)
        qaug_scr[2 * hh, 0:LANES, :] = jnp.where(row < DIFF_HEAD_DIM, qs, zero)
        qaug_scr[2 * hh + 1, 0:LANES, :] = jnp.where(row >= DIFF_HEAD_DIM, qs, zero)
        qaug_scr[2 * hh, LANES:2 * LANES, :] = srow
        qaug_scr[2 * hh + 1, LANES:2 * LANES, :] = srow
    m_scr[...] = jnp.full_like(m_scr, NEG_INF)
    acc_scr[...] = jnp.zeros_like(acc_scr)

    def qk(hh, j):
        for s in range(nh):
            rows = pl.ds(pl.multiple_of(j * tq + s * tk, tk), tk)
            kc = jnp.concatenate([k_ref[rows, hh * LANES:(hh + 1) * LANES],
                                  e_ref[s * tk:(s + 1) * tk, :]], axis=1)
            for c in range(2):
                z = _dot(kc, qaug_scr[2 * hh + c])
                z_scrs[hh][c * nh + s] = z
                zmax_scrs[hh][c * nh + s] = jnp.max(z, axis=0, keepdims=True)

    def softmax_pv(hh, j, diagonal):
        vt = jnp.concatenate([v_ref[(tq // LANES) * j + s, hh * LANES:(hh + 1) * LANES, :]
                              for s in range(tq // LANES)], axis=1)
        vt = jnp.concatenate([vt, jnp.ones((DIFF_ACC_ROWS - DIFF_V_DIM, tq), BF16)], axis=0)
        koff = (jnp.zeros((1, tq), jnp.int32) + j * tq).astype(F32) * slopes[hh]
        kr = lax.broadcasted_iota(jnp.int32, (tk, tq), 0)
        qc = lax.broadcasted_iota(jnp.int32, (tk, tq), 1)
        for c in range(2):
            zs = [z_scrs[hh][c * nh + s] for s in range(nh)]
            if diagonal:
                zs = [jnp.where(kr + s * tk <= qc, z, NEG_INF) for s, z in enumerate(zs)]
                zmaxs = [jnp.max(z, axis=0, keepdims=True) for z in zs]
            else:
                zmaxs = [zmax_scrs[hh][c * nh + s] for s in range(nh)]
            m_old = m_scr[2 * hh + c]
            m_tile = functools.reduce(jnp.maximum, zmaxs)
            m_new = jnp.maximum(m_old, m_tile + koff)
            alpha = jnp.exp2(m_old - m_new)
            shift = m_new - koff
            p = jnp.concatenate([jnp.exp2(z - shift).astype(BF16) for z in zs], axis=0)
            acc_scr[2 * hh + c] = alpha * acc_scr[2 * hh + c] + _dot(vt, p)
            m_scr[2 * hh + c] = m_new

    group_a = range(0, DIFF_HPS // 2)
    group_b = range(DIFF_HPS // 2, DIFF_HPS)
    for hh in group_a:
        qk(hh, 0)

    def body(j, carry):
        for hh in group_b:
            qk(hh, j)
        for hh in group_a:
            softmax_pv(hh, j, False)
        for hh in group_a:
            qk(hh, j + 1)
        for hh in group_b:
            softmax_pv(hh, j, False)
        return carry

    lax.fori_loop(0, i, body, 0)
    for hh in group_b:
        qk(hh, i)
    for hh in range(DIFF_HPS):
        softmax_pv(hh, i, True)

    lp = lam_ref[...]
    lam = (jnp.exp(jnp.sum(lp[0:1, :] * lp[1:2, :], axis=1, keepdims=True))
           - jnp.exp(jnp.sum(lp[2:3, :] * lp[3:4, :], axis=1, keepdims=True)) + lam_init)
    for hh in range(DIFF_HPS):
        a0, a1 = acc_scr[2 * hh], acc_scr[2 * hh + 1]
        o0 = a0[0:DIFF_V_DIM, :] * (1.0 / a0[DIFF_V_DIM:DIFF_V_DIM + 1, :])
        o1 = a1[0:DIFF_V_DIM, :] * (1.0 / a1[DIFF_V_DIM:DIFF_V_DIM + 1, :])
        o = o0 - lam * o1
        y = o * lax.rsqrt(jnp.mean(o * o, axis=0, keepdims=True) + NORM_EPS)
        o_ref[:, hh * DIFF_V_DIM:(hh + 1) * DIFF_V_DIM] = ((y.T * g_ref[...]) * (1.0 - lam_init)).astype(BF16)


def _diff_attn(slopes, lam_params, subln, proj, proj_t, *, lam_init):
    tq, tk, hps = DIFF_TQ, DIFF_TK, DIFF_HPS
    w = hps * LANES
    k_col0 = N_GATE // w
    v_row0 = DIFF_WIDTH // w
    r = np.arange(tq)[:, None]
    c = np.arange(LANES)[None, :]
    e = jnp.asarray(np.where(c < 2, r % tk, np.where(c < 4, r // tk, 0)), dtype=BF16)
    return pl.pallas_call(
        functools.partial(_diff_body, lam_init=lam_init),
        grid=(DIFF_HEADS // hps, SEQ // tq),
        in_specs=[
            pl.BlockSpec(memory_space=pltpu.SMEM),
            pl.BlockSpec((4, DIFF_HEAD_DIM), lambda h, i: (0, 0)),
            pl.BlockSpec((tq // LANES, w, LANES), lambda h, i: (i, h, 0)),
            pl.BlockSpec((SEQ, w), lambda h, i: (0, k_col0 + h)),
            pl.BlockSpec((N_SLABS, w, LANES), lambda h, i: (0, v_row0 + h, 0)),
            pl.BlockSpec((tq, LANES), lambda h, i: (0, 0)),
            pl.BlockSpec((1, DIFF_V_DIM), lambda h, i: (0, 0)),
        ],
        out_specs=pl.BlockSpec((tq, hps * DIFF_V_DIM), lambda h, i: (i, h)),
        out_shape=jax.ShapeDtypeStruct((SEQ, DIFF_WIDTH), BF16),
        scratch_shapes=[
            pltpu.VMEM((2 * hps, 2 * LANES, tq), BF16),
            *[pltpu.VMEM((2 * (tq // tk), tk, tq), F32)] * hps,
            *[pltpu.VMEM((2 * (tq // tk), 1, tq), F32)] * hps,
            pltpu.VMEM((2 * hps, 1, tq), F32),
            pltpu.VMEM((2 * hps, DIFF_ACC_ROWS, tq), F32),
        ],
        compiler_params=_params("arbitrary", "arbitrary"),
        name="diff_attn",
    )(slopes, lam_params, proj_t, proj, proj_t, e, subln)


SWA_TQ = 512
SWA_BLOCKS = SWA_TQ // WINDOW


def _swa_bias(slope, off):
    kb = lax.broadcasted_iota(jnp.int32, (2 * WINDOW, WINDOW), 0)
    qa = lax.broadcasted_iota(jnp.int32, (2 * WINDOW, WINDOW), 1)
    dist = off + qa - kb
    return jnp.where((dist >= 0) & (dist < WINDOW), -slope * dist.astype(F32), NEG_INF)


def _swa_body(slopes_ref, sinks_ref, q_ref, k_ref, v_ref, o_ref, qaug_scr, bias_scr):
    i = pl.program_id(0)

    @pl.when(i == 0)
    def _():
        qaug_scr[...] = jnp.zeros_like(qaug_scr)
        for hd in range(SWA_Q_HEADS):
            bias_scr[hd] = _swa_bias(slopes_ref[hd], WINDOW)

    def block(t, slab0, bias_of):
        kwin = k_ref[pl.ds(pl.multiple_of(slab0 * WINDOW, WINDOW), 2 * WINDOW), :]
        vt = jnp.concatenate([v_ref[slab0], v_ref[slab0 + 1]], axis=1)
        ones = jnp.ones((16, 2 * WINDOW), BF16)
        qs = q_ref[t]
        zs = []
        for kh in range(SWA_KV_HEADS):
            r0 = kh * SWA_HEAD_DIM
            for g in range(SWA_GROUP):
                hd = kh * SWA_GROUP + g
                qh = qs[hd * SWA_HEAD_DIM:(hd + 1) * SWA_HEAD_DIM, :].astype(F32) * (SWA_HEAD_DIM ** -0.5 * LOG2E)
                qaug_scr[kh, r0:r0 + SWA_HEAD_DIM, g * LANES:(g + 1) * LANES] = qh.astype(BF16)
            zs.append(_dot(kwin, qaug_scr[kh]))
        outs = []
        for kh in range(SWA_KV_HEADS):
            r0 = kh * SWA_HEAD_DIM
            z = zs[kh]
            vt_kh = jnp.concatenate([vt[r0:r0 + SWA_HEAD_DIM, :], ones], axis=0)
            row = []
            for g in range(SWA_GROUP):
                hd = kh * SWA_GROUP + g
                sink = sinks_ref[hd]
                sc = z[:, g * LANES:(g + 1) * LANES] + bias_of(hd)
                m = jnp.maximum(jnp.max(sc, axis=0, keepdims=True), sink)
                p = jnp.exp2(sc - m).astype(BF16)
                ot = _dot(vt_kh, p)
                denom = ot[SWA_HEAD_DIM:SWA_HEAD_DIM + 1, :] + jnp.exp2(sink - m)
                row.append(ot[0:SWA_HEAD_DIM, :] * (1.0 / denom))
            outs.append(row)
        r = pl.multiple_of(t * WINDOW, WINDOW)
        for pr in range(SWA_KV_HEADS // 2):
            for g in range(SWA_GROUP):
                x = jnp.concatenate([outs[2 * pr][g], outs[2 * pr + 1][g]], axis=0)
                c0 = (pr * SWA_GROUP + g) * LANES
                o_ref[pl.ds(r, WINDOW), c0:c0 + LANES] = x.T.astype(BF16)

    @pl.when(i == 0)
    def _():
        block(0, 0, lambda hd: _swa_bias(slopes_ref[hd], 0))

    def body(t, carry):
        block(t, i * SWA_BLOCKS + t - 1, lambda hd: bias_scr[hd])
        return carry

    lax.fori_loop(jnp.where(i == 0, 1, 0), SWA_BLOCKS, body, 0)


def _swa_attn(slopes, sinks, proj, proj_t):
    q_row_blk = (2 * DIFF_WIDTH) // SWA_WIDTH
    k_col_blk = (N_GATE + DIFF_WIDTH) // SWA_KV_WIDTH
    v_row_blk = (2 * DIFF_WIDTH + SWA_WIDTH) // SWA_KV_WIDTH + 1
    return pl.pallas_call(
        _swa_body,
        grid=(SEQ // SWA_TQ,),
        in_specs=[
            pl.BlockSpec(memory_space=pltpu.SMEM),
            pl.BlockSpec(memory_space=pltpu.SMEM),
            pl.BlockSpec((SWA_BLOCKS, SWA_WIDTH, LANES), lambda i: (i, q_row_blk, 0)),
            pl.BlockSpec((SEQ, SWA_KV_WIDTH), lambda i: (0, k_col_blk)),
            pl.BlockSpec((N_SLABS, SWA_KV_WIDTH, LANES), lambda i: (0, v_row_blk, 0)),
        ],
        out_specs=pl.BlockSpec((SWA_TQ, SWA_WIDTH), lambda i: (i, 0)),
        out_shape=jax.ShapeDtypeStruct((SEQ, SWA_WIDTH), BF16),
        scratch_shapes=[pltpu.VMEM((SWA_KV_HEADS, SWA_KV_WIDTH, SWA_GROUP * LANES), BF16),
                        pltpu.VMEM((SWA_Q_HEADS, 2 * WINDOW, WINDOW), F32)],
        compiler_params=_params("arbitrary"),
        name="swa_attn",
    )(slopes, sinks, proj_t, proj, proj_t)


def _mix_body(a_ref, b_ref, ga_ref, gb_ref, wa_ref, wb_ref, wo_ref, x_ref, gpost_ref, o_ref):
    j = pl.program_id(1)

    @pl.when(j == 0)
    def _():
        o_ref[...] = jnp.zeros_like(o_ref)

    ya = _dot(a_ref[...], wa_ref[...])
    yb = _dot(b_ref[...], wb_ref[...])
    mixed = ga_ref[...].astype(F32) * ya + gb_ref[...].astype(F32) * yb
    o_ref[...] += _dot(mixed.astype(BF16), wo_ref[...])

    @pl.when(j == pl.num_programs(1) - 1)
    def _():
        o_ref[...] = x_ref[...] + _rms(o_ref[...]) * gpost_ref[...]


def _mix_out(a, b, proj, wa, wb, wo, x, gpost, *, tm=512, tn=1024):
    s, d = x.shape
    nj = d // tn
    return pl.pallas_call(
        _mix_body,
        grid=(s // tm, nj),
        in_specs=[
            pl.BlockSpec((tm, DIFF_WIDTH), lambda i, j: (i, 0)),
            pl.BlockSpec((tm, SWA_WIDTH), lambda i, j: (i, 0)),
            pl.BlockSpec((tm, tn), lambda i, j: (i, j)),
            pl.BlockSpec((tm, tn), lambda i, j: (i, nj + j)),
            pl.BlockSpec((DIFF_WIDTH, tn), lambda i, j: (0, j)),
            pl.BlockSpec((SWA_WIDTH, tn), lambda i, j: (0, j)),
            pl.BlockSpec((tn, d), lambda i, j: (j, 0)),
            pl.BlockSpec((tm, d), lambda i, j: (i, 0)),
            pl.BlockSpec((1, d), lambda i, j: (0, 0)),
        ],
        out_specs=pl.BlockSpec((tm, d), lambda i, j: (i, 0)),
        out_shape=jax.ShapeDtypeStruct((s, d), F32),
        compiler_params=_params("arbitrary", "arbitrary"),
        name="mix_out",
    )(a, b, proj, proj, wa, wb, wo, x, gpost)


def _alibi_slopes(n):
    return jnp.asarray(2.0 ** (-8.0 * np.arange(1, n + 1) / n), dtype=F32)


def kernel(x, ffn1_norm_pre, ffn1_w_in, ffn1_w_out, ffn1_norm_post, mix_norm_pre, w_in, diff_lambda, diff_subln, swa_sinks, w_branch_diff, w_branch_swa, w_out, mix_norm_post, ffn2_norm_pre, ffn2_w_in, ffn2_w_out, ffn2_norm_post):
    depth = ffn1_w_in.shape[0]
    xs = x.reshape(SEQ, D_MODEL)
    for l in range(depth):
        lam_init = 0.8 - 0.6 * math.exp(-0.3 * l)
        w = w_in[l]
        wbs = (w_branch_swa[l].reshape(2, 2, SWA_GROUP, SWA_HEAD_DIM, D_MODEL)
               .transpose(0, 2, 1, 3, 4).reshape(SWA_WIDTH, D_MODEL).astype(BF16))
        row = lambda v: v.reshape(1, -1)

        x1, h2, w2_in, w2_out, wbd, wo = _ffn(
            xs, row(ffn1_norm_pre[l]), ffn1_w_in[l].astype(BF16), ffn1_w_out[l].astype(BF16),
            row(ffn1_norm_post[l]), row(mix_norm_pre[l]),
            convert=(ffn2_w_in[l], ffn2_w_out[l], w_branch_diff[l], w_out[l]), tm=512)
        proj = _proj(h2, w)
        proj_t = _proj_t(h2, w)
        a = _diff_attn(_alibi_slopes(DIFF_HEADS) * LOG2E, diff_lambda[l], row(diff_subln[l]), proj, proj_t,
                       lam_init=lam_init)
        b = _swa_attn(_alibi_slopes(SWA_Q_HEADS) * LOG2E, swa_sinks[l] * LOG2E, proj, proj_t)
        x2 = _mix_out(a, b, proj, wbd, wbs, wo, x1, row(mix_norm_post[l]))
        (xs,) = _ffn(x2, row(ffn2_norm_pre[l]), w2_in, w2_out, row(ffn2_norm_post[l]))
    return xs.reshape(x.shape)
```

```python
import functools
import math

import numpy as np
import jax
import jax.numpy as jnp
from jax import lax
from jax.experimental import pallas as pl
from jax.experimental.pallas import tpu as pltpu

D_MODEL = 2048
SEQ = 8192
DIFF_HEADS = 8
DIFF_HEAD_DIM = 64
DIFF_V_DIM = 128
DIFF_WIDTH = 1024
SWA_Q_HEADS = 16
SWA_KV_HEADS = 4
SWA_GROUP = 4
SWA_HEAD_DIM = 64
SWA_WIDTH = 1024
SWA_KV_WIDTH = 256
WINDOW = 128
D_FF = 5504
NORM_EPS = 1e-6
NEG_INF = -1e30

LANES = 128
MXU_DIM = 256
VMEM_LIMIT_BYTES = 60 * 1024 * 1024

N_GATE = 2 * D_MODEL
PROJ_COLS = N_GATE + DIFF_WIDTH + 2 * SWA_KV_WIDTH
PROJT_ROWS = DIFF_WIDTH + DIFF_WIDTH + SWA_WIDTH + 2 * SWA_KV_WIDTH
N_SLABS = SEQ // LANES

BF16 = jnp.bfloat16
F32 = jnp.float32


def _dot(a, b):
    return jnp.dot(a, b, preferred_element_type=F32)


def _rms(x):
    return x * lax.rsqrt(jnp.mean(x * x, axis=-1, keepdims=True) + NORM_EPS)


def _params(*sem, flags=None):
    return pltpu.CompilerParams(dimension_semantics=sem, vmem_limit_bytes=VMEM_LIMIT_BYTES, flags=flags)


def _rider_specs(mats, n_steps, step_of):
    specs = []
    for m in mats:
        rb = 16 * pl.cdiv(m.shape[0], 16 * n_steps)
        while m.shape[0] % rb:
            rb += 16
        specs.append(pl.BlockSpec((rb, m.shape[1]),
                                  lambda *ids, nb=m.shape[0] // rb: (jnp.minimum(step_of(*ids), nb - 1), 0)))
    return specs


def _round_riders(srcs, dsts):
    for src, dst in zip(srcs, dsts):
        dst[...] = src[...].astype(BF16)


FFN_ROW_CHUNK = 512


def _ffn_body(*refs, emit_next, overlap, n_convert):
    refs = list(refs)
    x_ref, gpre_ref, wg_ref, wu_ref, wo_ref, gpost_ref = refs[:6]
    del refs[:6]
    gnext_ref = refs.pop(0) if emit_next else None
    cvt_in = [refs.pop(0) for _ in range(n_convert)]
    o_ref = refs.pop(0)
    hn_ref = refs.pop(0) if emit_next else None
    cvt_out = [refs.pop(0) for _ in range(n_convert)]
    (h_scr,) = refs
    j = pl.program_id(1)
    last = pl.num_programs(1) - 1

    @pl.when(j == 0)
    def _():
        h_scr[...] = (_rms(x_ref[...]) * gpre_ref[...]).astype(BF16)
        o_ref[...] = jnp.zeros_like(o_ref)

    _round_riders(cvt_in, cvt_out)

    for r in range(0, h_scr.shape[0], FFN_ROW_CHUNK):
        rows = slice(r, r + FFN_ROW_CHUNK)
        h = h_scr[rows, :]
        g = _dot(h, wg_ref[...])
        u = _dot(h, wu_ref[...])
        act = jax.nn.silu(g) * u
        if overlap:
            col = lax.broadcasted_iota(jnp.int32, act.shape, 1)
            act = jnp.where((j == last) & (col < overlap), 0.0, act)
        o_ref[rows, :] += _dot(act.astype(BF16), wo_ref[...])

    @pl.when(j == last)
    def _():
        x1 = x_ref[...] + 0.5 * (_rms(o_ref[...]) * gpost_ref[...])
        o_ref[...] = x1
        if emit_next:
            hn_ref[...] = (_rms(x1) * gnext_ref[...]).astype(BF16)


def _ffn(x, gpre, w_in, w_out, gpost, gnext=None, convert=(), *, tm=1024, tf=512):
    emit_next = gnext is not None
    s, d = x.shape
    d_ff = w_out.shape[0]
    nj = pl.cdiv(d_ff, tf)
    overlap = nj * tf - d_ff
    assert tf % LANES == 0 and d_ff % LANES == 0
    start = lambda j, base=0: (jnp.minimum(j * (tf // LANES), (d_ff - tf) // LANES) + base // LANES) * LANES
    row = lambda i, j: (i, 0)
    fixed = lambda i, j: (0, 0)
    in_specs = [
        pl.BlockSpec((tm, d), row),
        pl.BlockSpec((1, d), fixed),
        pl.BlockSpec((pl.Element(d), pl.Element(tf)), lambda i, j: (0, start(j))),
        pl.BlockSpec((pl.Element(d), pl.Element(tf)), lambda i, j: (0, start(j, d_ff))),
        pl.BlockSpec((pl.Element(tf), pl.Element(d)), lambda i, j: (start(j), 0)),
        pl.BlockSpec((1, d), fixed),
    ]
    args = [x, gpre, w_in, w_in, w_out, gpost]
    out_shape = [jax.ShapeDtypeStruct((s, d), F32)]
    out_specs = [pl.BlockSpec((tm, d), row)]
    if emit_next:
        in_specs.append(pl.BlockSpec((1, d), fixed))
        args.append(gnext)
        out_shape.append(jax.ShapeDtypeStruct((s, d), BF16))
        out_specs.append(pl.BlockSpec((tm, d), row))
    riders = _rider_specs(convert, (s // tm) * nj, lambda i, j: i * nj + j)
    in_specs += riders
    args += list(convert)
    out_specs += riders
    out_shape += [jax.ShapeDtypeStruct(m.shape, BF16) for m in convert]
    res = pl.pallas_call(
        functools.partial(_ffn_body, emit_next=emit_next, overlap=overlap, n_convert=len(convert)),
        grid=(s // tm, nj),
        in_specs=in_specs,
        out_specs=out_specs,
        out_shape=out_shape,
        scratch_shapes=[pltpu.VMEM((tm, d), BF16)],
        compiler_params=_params("arbitrary", "arbitrary"),
        name="ffn_next" if emit_next else "ffn",
    )(*args)
    return res


PROJ_TN = 2 * MXU_DIM
assert 2 * SWA_KV_WIDTH == PROJ_TN
_O_DQ, _O_DK, _O_DV, _O_SQ = (k * DIFF_WIDTH // PROJ_TN for k in range(4))
_O_SKV = _O_SQ + SWA_WIDTH // PROJ_TN
_O_G = _O_SKV + 1
_N_G, _N_D, _N_S = N_GATE // PROJ_TN, DIFF_WIDTH // PROJ_TN, SWA_WIDTH // PROJ_TN


def _proj_src_tile(t):
    return jnp.where(t < _N_G, _O_G + t, jnp.where(t < _N_G + _N_D, _O_DK + (t - _N_G), _O_SKV))


def _proj_t_src_tile(t):
    return jnp.where(t < _N_D, _O_DQ + t,
                     jnp.where(t < 2 * _N_D, _O_DV + (t - _N_D),
                               jnp.where(t < 2 * _N_D + _N_S, _O_SQ + (t - 2 * _N_D), _O_SKV)))


PROJ_ROW_CHUNK = 256


def _proj_body(h_ref, w_ref, *refs):
    n = (len(refs) - 2) // 2
    cvt_in, o_ref, cvt_out, w_scr = refs[:n], refs[n], refs[n + 1:2 * n + 1], refs[2 * n + 1]

    def run(gated):
        w_scr[...] = w_ref[...].astype(BF16)
        _round_riders(cvt_in, cvt_out)
        for r in range(0, h_ref.shape[0], PROJ_ROW_CHUNK):
            rows = slice(r, r + PROJ_ROW_CHUNK)
            res = _dot(h_ref[rows, :], w_scr[...])
            if gated:
                res = 0.5 * jnp.tanh(0.5 * res) + 0.5
            o_ref[rows, :] = res.astype(BF16)

    gated = pl.program_id(1) < _N_G
    pl.when(gated)(lambda: run(True))
    pl.when(jnp.logical_not(gated))(lambda: run(False))


def _proj(h, w, convert=(), *, tm=2048):
    s, d = h.shape
    tn = PROJ_TN
    nt = PROJ_COLS // tn
    riders = _rider_specs(convert, (s // tm) * nt, lambda i, t: i * nt + t)
    return pl.pallas_call(
        _proj_body,
        grid=(s // tm, nt),
        in_specs=[pl.BlockSpec((tm, d), lambda i, t: (i, 0)),
                  pl.BlockSpec((d, tn), lambda i, t: (0, _proj_src_tile(t)))] + riders,
        out_specs=[pl.BlockSpec((tm, tn), lambda i, t: (i, t))] + riders,
        out_shape=[jax.ShapeDtypeStruct((s, PROJ_COLS), BF16)]
                  + [jax.ShapeDtypeStruct(m.shape, BF16) for m in convert],
        scratch_shapes=[pltpu.VMEM((d, tn), BF16)],
        compiler_params=_params("arbitrary", "arbitrary"),
        name="proj",
    )(h, w, *convert)


def _proj_t_body(h_ref, w_ref, o_ref, w_scr):
    w_scr[...] = w_ref[...].astype(BF16)
    for r in range(0, h_ref.shape[0], PROJ_ROW_CHUNK):
        res = _dot(h_ref[r:r + PROJ_ROW_CHUNK, :], w_scr[...])
        for k in range(PROJ_ROW_CHUNK // LANES):
            o_ref[r // LANES + k] = res[k * LANES:(k + 1) * LANES, :].T.astype(BF16)


def _proj_t(h, w, *, tm=2048):
    s, d = h.shape
    tn = PROJ_TN
    return pl.pallas_call(
        _proj_t_body,
        grid=(s // tm, PROJT_ROWS // tn),
        in_specs=[pl.BlockSpec((tm, d), lambda i, t: (i, 0)),
                  pl.BlockSpec((d, tn), lambda i, t: (0, _proj_t_src_tile(t)))],
        out_specs=pl.BlockSpec((tm // LANES, tn, LANES), lambda i, t: (i, t, 0)),
        out_shape=jax.ShapeDtypeStruct((s // LANES, PROJT_ROWS, LANES), BF16),
        scratch_shapes=[pltpu.VMEM((d, tn), BF16)],
        compiler_params=_params("arbitrary", "arbitrary"),
        name="proj_t",
    )(h, w)


DIFF_TQ = 512
DIFF_TK = 256
DIFF_ACC_ROWS = DIFF_V_DIM + 16
DIFF_HPS = 4
LOG2E = math.log2(math.e)


def _diff_body(slopes_ref, lam_ref, q_ref, k_ref, v_ref, e_ref, g_ref, o_ref, qaug_scr, *scr, lam_init):
    tq, tk, nh = DIFF_TQ, DIFF_TK, DIFF_TQ // DIFF_TK
    hp = pl.program_id(0)
    i = pl.program_id(1)
    z_scrs, zmax_scrs = scr[:DIFF_HPS], scr[DIFF_HPS:2 * DIFF_HPS]
    m_scr, acc_scr = scr[2 * DIFF_HPS:]
    slopes = [slopes_ref[hp * DIFF_HPS + hh] for hh in range(DIFF_HPS)]

    row = lax.broadcasted_iota(jnp.int32, (LANES, tq), 0)
    for hh in range(DIFF_HPS):
        q = jnp.concatenate([q_ref[s, hh * LANES:(hh + 1) * LANES, :] for s in range(tq // LANES)], axis=1)
        qs = (q.astype(F32) * (DIFF_HEAD_DIM ** -0.5 * LOG2E)).astype(BF16)
        zero = jnp.zeros_like(qs)
        s_f32 = jnp.where(row < 2, slopes[hh], jnp.where(row < 4, slopes[hh] * tk, 0.0))
        s_hi = s_f32.astype(BF16)
        s_lo = (s_f32 - s_hi.astype(F32)).astype(BF16)
        srow = jnp.where(row % 2 == 0, s_hi, s_lo)
        qaug_scr[2 * hh, 0:LANES, :] = jnp.where(row < DIFF_HEAD_DIM, qs, zero)
        qaug_scr[2 * hh + 1, 0:LANES, :] = jnp.where(row >= DIFF_HEAD_DIM, qs, zero)
        qaug_scr[2 * hh, LANES:2 * LANES, :] = srow
        qaug_scr[2 * hh + 1, LANES:2 * LANES, :] = srow
    m_scr[...] = jnp.full_like(m_scr, NEG_INF)
    acc_scr[...] = jnp.zeros_like(acc_scr)

    def qk(hh, j):
        for s in range(nh):
            rows = pl.ds(pl.multiple_of(j * tq + s * tk, tk), tk)
            kc = jnp.concatenate([k_ref[rows, hh * LANES:(hh + 1) * LANES],
                                  e_ref[s * tk:(s + 1) * tk, :]], axis=1)
            for c in range(2):
                z = _dot(kc, qaug_scr[2 * hh + c])
                z_scrs[hh][c * nh + s] = z
                zmax_scrs[hh][c * nh + s] = jnp.max(z, axis=0, keepdims=True)

    def softmax_pv(hh, j, diagonal):
        vt = jnp.concatenate([v_ref[(tq // LANES) * j + s, hh * LANES:(hh + 1) * LANES, :]
                              for s in range(tq // LANES)], axis=1)
        vt = jnp.concatenate([vt, jnp.ones((DIFF_ACC_ROWS - DIFF_V_DIM, tq), BF16)], axis=0)
        koff = (jnp.zeros((1, tq), jnp.int32) + j * tq).astype(F32) * slopes[hh]
        kr = lax.broadcasted_iota(jnp.int32, (tk, tq), 0)
        qc = lax.broadcasted_iota(jnp.int32, (tk, tq), 1)
        for c in range(2):
            zs = [z_scrs[hh][c * nh + s] for s in range(nh)]
            if diagonal:
                zs = [jnp.where(kr + s * tk <= qc, z, NEG_INF) for s, z in enumerate(zs)]
                zmaxs = [jnp.max(z, axis=0, keepdims=True) for z in zs]
            else:
                zmaxs = [zmax_scrs[hh][c * nh + s] for s in range(nh)]
            m_old = m_scr[2 * hh + c]
            m_tile = functools.reduce(jnp.maximum, zmaxs)
            m_new = jnp.maximum(m_old, m_tile + koff)
            alpha = jnp.exp2(m_old - m_new)
            shift = m_new - koff
            p = jnp.concatenate([jnp.exp2(z - shift).astype(BF16) for z in zs], axis=0)
            acc_scr[2 * hh + c] = alpha * acc_scr[2 * hh + c] + _dot(vt, p)
            m_scr[2 * hh + c] = m_new

    group_a = range(0, DIFF_HPS // 2)
    group_b = range(DIFF_HPS // 2, DIFF_HPS)
    for hh in group_a:
        qk(hh, 0)

    def body(j, carry):
        for hh in group_b:
            qk(hh, j)
        for hh in group_a:
            softmax_pv(hh, j, False)
        for hh in group_a:
            qk(hh, j + 1)
        for hh in group_b:
            softmax_pv(hh, j, False)
        return carry

    lax.fori_loop(0, i, body, 0)
    for hh in group_b:
        qk(hh, i)
    for hh in range(DIFF_HPS):
        softmax_pv(hh, i, True)

    lp = lam_ref[...]
    lam = (jnp.exp(jnp.sum(lp[0:1, :] * lp[1:2, :], axis=1, keepdims=True))
           - jnp.exp(jnp.sum(lp[2:3, :] * lp[3:4, :], axis=1, keepdims=True)) + lam_init)
    for hh in range(DIFF_HPS):
        a0, a1 = acc_scr[2 * hh], acc_scr[2 * hh + 1]
        o0 = a0[0:DIFF_V_DIM, :] * (1.0 / a0[DIFF_V_DIM:DIFF_V_DIM + 1, :])
        o1 = a1[0:DIFF_V_DIM, :] * (1.0 / a1[DIFF_V_DIM:DIFF_V_DIM + 1, :])
        o = o0 - lam * o1
        y = o * lax.rsqrt(jnp.mean(o * o, axis=0, keepdims=True) + NORM_EPS)
        o_ref[:, hh * DIFF_V_DIM:(hh + 1) * DIFF_V_DIM] = ((y.T * g_ref[...]) * (1.0 - lam_init)).astype(BF16)


def _diff_attn(slopes, lam_params, subln, proj, proj_t, *, lam_init):
    tq, tk, hps = DIFF_TQ, DIFF_TK, DIFF_HPS
    w = hps * LANES
    k_col0 = N_GATE // w
    v_row0 = DIFF_WIDTH // w
    r = np.arange(tq)[:, None]
    c = np.arange(LANES)[None, :]
    e = jnp.asarray(np.where(c < 2, r % tk, np.where(c < 4, r // tk, 0)), dtype=BF16)
    return pl.pallas_call(
        functools.partial(_diff_body, lam_init=lam_init),
        grid=(DIFF_HEADS // hps, SEQ // tq),
        in_specs=[
            pl.BlockSpec(memory_space=pltpu.SMEM),
            pl.BlockSpec((4, DIFF_HEAD_DIM), lambda h, i: (0, 0)),
            pl.BlockSpec((tq // LANES, w, LANES), lambda h, i: (i, h, 0)),
            pl.BlockSpec((SEQ, w), lambda h, i: (0, k_col0 + h)),
            pl.BlockSpec((N_SLABS, w, LANES), lambda h, i: (0, v_row0 + h, 0)),
            pl.BlockSpec((tq, LANES), lambda h, i: (0, 0)),
            pl.BlockSpec((1, DIFF_V_DIM), lambda h, i: (0, 0)),
        ],
        out_specs=pl.BlockSpec((tq, hps * DIFF_V_DIM), lambda h, i: (i, h)),
        out_shape=jax.ShapeDtypeStruct((SEQ, DIFF_WIDTH), BF16),
        scratch_shapes=[
            pltpu.VMEM((2 * hps, 2 * LANES, tq), BF16),
            *[pltpu.VMEM((2 * (tq // tk), tk, tq), F32)] * hps,
            *[pltpu.VMEM((2 * (tq // tk), 1, tq), F32)] * hps,
            pltpu.VMEM((2 * hps, 1, tq), F32),
            pltpu.VMEM((2 * hps, DIFF_ACC_ROWS, tq), F32),
        ],
        compiler_params=_params("arbitrary", "arbitrary"),
        name="diff_attn",
    )(slopes, lam_params, proj_t, proj, proj_t, e, subln)


SWA_TQ = 512
SWA_BLOCKS = SWA_TQ // WINDOW


def _swa_bias(slope, off):
    kb = lax.broadcasted_iota(jnp.int32, (2 * WINDOW, WINDOW), 0)
    qa = lax.broadcasted_iota(jnp.int32, (2 * WINDOW, WINDOW), 1)
    dist = off + qa - kb
    return jnp.where((dist >= 0) & (dist < WINDOW), -slope * dist.astype(F32), NEG_INF)


def _swa_body(slopes_ref, sinks_ref, q_ref, k_ref, v_ref, o_ref, qaug_scr, bias_scr):
    i = pl.program_id(0)

    @pl.when(i == 0)
    def _():
        qaug_scr[...] = jnp.zeros_like(qaug_scr)
        for hd in range(SWA_Q_HEADS):
            bias_scr[hd] = _swa_bias(slopes_ref[hd], WINDOW)

    def block(t, slab0, bias_of):
        kwin = k_ref[pl.ds(pl.multiple_of(slab0 * WINDOW, WINDOW), 2 * WINDOW), :]
        vt = jnp.concatenate([v_ref[slab0], v_ref[slab0 + 1]], axis=1)
        ones = jnp.ones((16, 2 * WINDOW), BF16)
        qs = q_ref[t]
        zs = []
        for kh in range(SWA_KV_HEADS):
            r0 = kh * SWA_HEAD_DIM
            for g in range(SWA_GROUP):
                hd = kh * SWA_GROUP + g
                qh = qs[hd * SWA_HEAD_DIM:(hd + 1) * SWA_HEAD_DIM, :].astype(F32) * (SWA_HEAD_DIM ** -0.5 * LOG2E)
                qaug_scr[kh, r0:r0 + SWA_HEAD_DIM, g * LANES:(g + 1) * LANES] = qh.astype(BF16)
            zs.append(_dot(kwin, qaug_scr[kh]))
        outs = []
        for kh in range(SWA_KV_HEADS):
            r0 = kh * SWA_HEAD_DIM
            z = zs[kh]
            vt_kh = jnp.concatenate([vt[r0:r0 + SWA_HEAD_DIM, :], ones], axis=0)
            row = []
            for g in range(SWA_GROUP):
                hd = kh * SWA_GROUP + g
                sink = sinks_ref[hd]
                sc = z[:, g * LANES:(g + 1) * LANES] + bias_of(hd)
                m = jnp.maximum(jnp.max(sc, axis=0, keepdims=True), sink)
                p = jnp.exp2(sc - m).astype(BF16)
                ot = _dot(vt_kh, p)
                denom = ot[SWA_HEAD_DIM:SWA_HEAD_DIM + 1, :] + jnp.exp2(sink - m)
                row.append(ot[0:SWA_HEAD_DIM, :] * (1.0 / denom))
            outs.append(row)
        r = pl.multiple_of(t * WINDOW, WINDOW)
        for pr in range(SWA_KV_HEADS // 2):
            for g in range(SWA_GROUP):
                x = jnp.concatenate([outs[2 * pr][g], outs[2 * pr + 1][g]], axis=0)
                c0 = (pr * SWA_GROUP + g) * LANES
                o_ref[pl.ds(r, WINDOW), c0:c0 + LANES] = x.T.astype(BF16)

    @pl.when(i == 0)
    def _():
        block(0, 0, lambda hd: _swa_bias(slopes_ref[hd], 0))

    def body(t, carry):
        block(t, i * SWA_BLOCKS + t - 1, lambda hd: bias_scr[hd])
        return carry

    lax.fori_loop(jnp.where(i == 0, 1, 0), SWA_BLOCKS, body, 0)


def _swa_attn(slopes, sinks, proj, proj_t):
    q_row_blk = (2 * DIFF_WIDTH) // SWA_WIDTH
    k_col_blk = (N_GATE + DIFF_WIDTH) // SWA_KV_WIDTH
    v_row_blk = (2 * DIFF_WIDTH + SWA_WIDTH) // SWA_KV_WIDTH + 1
    return pl.pallas_call(
        _swa_body,
        grid=(SEQ // SWA_TQ,),
        in_specs=[
            pl.BlockSpec(memory_space=pltpu.SMEM),
            pl.BlockSpec(memory_space=pltpu.SMEM),
            pl.BlockSpec((SWA_BLOCKS, SWA_WIDTH, LANES), lambda i: (i, q_row_blk, 0)),
            pl.BlockSpec((SEQ, SWA_KV_WIDTH), lambda i: (0, k_col_blk)),
            pl.BlockSpec((N_SLABS, SWA_KV_WIDTH, LANES), lambda i: (0, v_row_blk, 0)),
        ],
        out_specs=pl.BlockSpec((SWA_TQ, SWA_WIDTH), lambda i: (i, 0)),
        out_shape=jax.ShapeDtypeStruct((SEQ, SWA_WIDTH), BF16),
        scratch_shapes=[pltpu.VMEM((SWA_KV_HEADS, SWA_KV_WIDTH, SWA_GROUP * LANES), BF16),
                        pltpu.VMEM((SWA_Q_HEADS, 2 * WINDOW, WINDOW), F32)],
        compiler_params=_params("arbitrary"),
        name="swa_attn",
    )(slopes, sinks, proj_t, proj, proj_t)


def _mix_body(a_ref, b_ref, ga_ref, gb_ref, wa_ref, wb_ref, wo_ref, x_ref, gpost_ref, o_ref):
    j = pl.program_id(1)

    @pl.when(j == 0)
    def _():
        o_ref[...] = jnp.zeros_like(o_ref)

    ya = _dot(a_ref[...], wa_ref[...])
    yb = _dot(b_ref[...], wb_ref[...])
    mixed = ga_ref[...].astype(F32) * ya + gb_ref[...].astype(F32) * yb
    o_ref[...] += _dot(mixed.astype(BF16), wo_ref[...])

    @pl.when(j == pl.num_programs(1) - 1)
    def _():
        o_ref[...] = x_ref[...] + _rms(o_ref[...]) * gpost_ref[...]


def _mix_out(a, b, proj, wa, wb, wo, x, gpost, *, tm=512, tn=1024):
    s, d = x.shape
    nj = d // tn
    return pl.pallas_call(
        _mix_body,
        grid=(s // tm, nj),
        in_specs=[
            pl.BlockSpec((tm, DIFF_WIDTH), lambda i, j: (i, 0)),
            pl.BlockSpec((tm, SWA_WIDTH), lambda i, j: (i, 0)),
            pl.BlockSpec((tm, tn), lambda i, j: (i, j)),
            pl.BlockSpec((tm, tn), lambda i, j: (i, nj + j)),
            pl.BlockSpec((DIFF_WIDTH, tn), lambda i, j: (0, j)),
            pl.BlockSpec((SWA_WIDTH, tn), lambda i, j: (0, j)),
            pl.BlockSpec((tn, d), lambda i, j: (j, 0)),
            pl.BlockSpec((tm, d), lambda i, j: (i, 0)),
            pl.BlockSpec((1, d), lambda i, j: (0, 0)),
        ],
        out_specs=pl.BlockSpec((tm, d), lambda i, j: (i, 0)),
        out_shape=jax.ShapeDtypeStruct((s, d), F32),
        compiler_params=_params("arbitrary", "arbitrary"),
        name="mix_out",
    )(a, b, proj, proj, wa, wb, wo, x, gpost)


def _alibi_slopes(n):
    return jnp.asarray(2.0 ** (-8.0 * np.arange(1, n + 1) / n), dtype=F32)


def kernel(x, ffn1_norm_pre, ffn1_w_in, ffn1_w_out, ffn1_norm_post, mix_norm_pre, w_in, diff_lambda, diff_subln, swa_sinks, w_branch_diff, w_branch_swa, w_out, mix_norm_post, ffn2_norm_pre, ffn2_w_in, ffn2_w_out, ffn2_norm_post):
    depth = ffn1_w_in.shape[0]
    xs = x.reshape(SEQ, D_MODEL)
    for l in range(depth):
        lam_init = 0.8 - 0.6 * math.exp(-0.3 * l)
        w = w_in[l]
        wbs = (w_branch_swa[l].reshape(2, 2, SWA_GROUP, SWA_HEAD_DIM, D_MODEL)
               .transpose(0, 2, 1, 3, 4).reshape(SWA_WIDTH, D_MODEL).astype(BF16))
        row = lambda v: v.reshape(1, -1)

        x1, h2, wbd, wo = _ffn(
            xs, row(ffn1_norm_pre[l]), ffn1_w_in[l].astype(BF16), ffn1_w_out[l].astype(BF16),
            row(ffn1_norm_post[l]), row(mix_norm_pre[l]), convert=(w_branch_diff[l], w_out[l]), tm=512)
        proj, w2_in, w2_out = _proj(h2, w, convert=(ffn2_w_in[l], ffn2_w_out[l]))
        proj_t = _proj_t(h2, w)
        a = _diff_attn(_alibi_slopes(DIFF_HEADS) * LOG2E, diff_lambda[l], row(diff_subln[l]), proj, proj_t,
                       lam_init=lam_init)
        b = _swa_attn(_alibi_slopes(SWA_Q_HEADS) * LOG2E, swa_sinks[l] * LOG2E, proj, proj_t)
        x2 = _mix_out(a, b, proj, wbd, wbs, wo, x1, row(mix_norm_post[l]))
        (xs,) = _ffn(x2, row(ffn2_norm_pre[l]), w2_in, w2_out, row(ffn2_norm_post[l]))
    return xs.reshape(x.shape)
```

```python
import functools
import math

import numpy as np
import jax
import jax.numpy as jnp
from jax import lax
from jax.experimental import pallas as pl
from jax.experimental.pallas import tpu as pltpu

D_MODEL = 2048
SEQ = 8192
DIFF_HEADS = 8
DIFF_HEAD_DIM = 64
DIFF_V_DIM = 128
DIFF_WIDTH = 1024
SWA_Q_HEADS = 16
SWA_KV_HEADS = 4
SWA_GROUP = 4
SWA_HEAD_DIM = 64
SWA_WIDTH = 1024
SWA_KV_WIDTH = 256
WINDOW = 128
D_FF = 5504
NORM_EPS = 1e-6
NEG_INF = -1e30

LANES = 128
MXU_DIM = 256
VMEM_LIMIT_BYTES = 60 * 1024 * 1024

N_GATE = 2 * D_MODEL
PROJ_COLS = N_GATE + DIFF_WIDTH + 2 * SWA_KV_WIDTH
PROJT_ROWS = DIFF_WIDTH + DIFF_WIDTH + SWA_WIDTH + 2 * SWA_KV_WIDTH
N_SLABS = SEQ // LANES

BF16 = jnp.bfloat16
F32 = jnp.float32


def _dot(a, b):
    return jnp.dot(a, b, preferred_element_type=F32)


def _rms(x):
    return x * lax.rsqrt(jnp.mean(x * x, axis=-1, keepdims=True) + NORM_EPS)


def _params(*sem, flags=None):
    return pltpu.CompilerParams(dimension_semantics=sem, vmem_limit_bytes=VMEM_LIMIT_BYTES, flags=flags)


def _rider_specs(mats, n_steps, step_of):
    specs = []
    for m in mats:
        rb = 16 * pl.cdiv(m.shape[0], 16 * n_steps)
        while m.shape[0] % rb:
            rb += 16
        specs.append(pl.BlockSpec((rb, m.shape[1]),
                                  lambda *ids, nb=m.shape[0] // rb: (jnp.minimum(step_of(*ids), nb - 1), 0)))
    return specs


def _round_riders(srcs, dsts):
    for src, dst in zip(srcs, dsts):
        dst[...] = src[...].astype(BF16)


FFN_ROW_CHUNK = 512
FFN_EDGE_ROWS = 256


def _ffn_body(*refs, emit_next, overlap, n_convert):
    refs = list(refs)
    x_ref, gpre_ref, wg_ref, wu_ref, wo_ref, gpost_ref = refs[:6]
    del refs[:6]
    gnext_ref = refs.pop(0) if emit_next else None
    cvt_in = [refs.pop(0) for _ in range(n_convert)]
    o_ref = refs.pop(0)
    hn_ref = refs.pop(0) if emit_next else None
    cvt_out = [refs.pop(0) for _ in range(n_convert)]
    (h_scr,) = refs
    j = pl.program_id(1)
    last = pl.num_programs(1) - 1
    tm = h_scr.shape[0]

    def step(chunk, first, final):
        _round_riders(cvt_in, cvt_out)
        for r in range(0, tm, chunk):
            rows = slice(r, r + chunk)
            if first:
                h = (_rms(x_ref[rows, :]) * gpre_ref[...]).astype(BF16)
                h_scr[rows, :] = h
            else:
                h = h_scr[rows, :]
            g = _dot(h, wg_ref[...])
            u = _dot(h, wu_ref[...])
            act = jax.nn.silu(g) * u
            if final and overlap:
                col = lax.broadcasted_iota(jnp.int32, act.shape, 1)
                act = jnp.where(col < overlap, 0.0, act)
            y = _dot(act.astype(BF16), wo_ref[...])
            if not first:
                y = o_ref[rows, :] + y
            if final:
                y = x_ref[rows, :] + 0.5 * (_rms(y) * gpost_ref[...])
                if emit_next:
                    hn_ref[rows, :] = (_rms(y) * gnext_ref[...]).astype(BF16)
            o_ref[rows, :] = y

    pl.when(j == 0)(lambda: step(FFN_EDGE_ROWS, True, False))
    pl.when((j > 0) & (j < last))(lambda: step(FFN_ROW_CHUNK, False, False))
    pl.when(j == last)(lambda: step(FFN_EDGE_ROWS, False, True))


def _ffn(x, gpre, w_in, w_out, gpost, gnext=None, convert=(), *, tm=1024, tf=512):
    emit_next = gnext is not None
    s, d = x.shape
    d_ff = w_out.shape[0]
    nj = pl.cdiv(d_ff, tf)
    overlap = nj * tf - d_ff
    assert tf % LANES == 0 and d_ff % LANES == 0
    start = lambda j, base=0: (jnp.minimum(j * (tf // LANES), (d_ff - tf) // LANES) + base // LANES) * LANES
    row = lambda i, j: (i, 0)
    fixed = lambda i, j: (0, 0)
    in_specs = [
        pl.BlockSpec((tm, d), row),
        pl.BlockSpec((1, d), fixed),
        pl.BlockSpec((pl.Element(d), pl.Element(tf)), lambda i, j: (0, start(j))),
        pl.BlockSpec((pl.Element(d), pl.Element(tf)), lambda i, j: (0, start(j, d_ff))),
        pl.BlockSpec((pl.Element(tf), pl.Element(d)), lambda i, j: (start(j), 0)),
        pl.BlockSpec((1, d), fixed),
    ]
    args = [x, gpre, w_in, w_in, w_out, gpost]
    out_shape = [jax.ShapeDtypeStruct((s, d), F32)]
    out_specs = [pl.BlockSpec((tm, d), row)]
    if emit_next:
        in_specs.append(pl.BlockSpec((1, d), fixed))
        args.append(gnext)
        out_shape.append(jax.ShapeDtypeStruct((s, d), BF16))
        out_specs.append(pl.BlockSpec((tm, d), row))
    riders = _rider_specs(convert, (s // tm) * nj, lambda i, j: i * nj + j)
    in_specs += riders
    args += list(convert)
    out_specs += riders
    out_shape += [jax.ShapeDtypeStruct(m.shape, BF16) for m in convert]
    res = pl.pallas_call(
        functools.partial(_ffn_body, emit_next=emit_next, overlap=overlap, n_convert=len(convert)),
        grid=(s // tm, nj),
        in_specs=in_specs,
        out_specs=out_specs,
        out_shape=out_shape,
        scratch_shapes=[pltpu.VMEM((tm, d), BF16)],
        compiler_params=_params("arbitrary", "arbitrary"),
        name="ffn_next" if emit_next else "ffn",
    )(*args)
    return res


PROJ_TN = 2 * MXU_DIM
assert 2 * SWA_KV_WIDTH == PROJ_TN
_O_DQ, _O_DK, _O_DV, _O_SQ = (k * DIFF_WIDTH // PROJ_TN for k in range(4))
_O_SKV = _O_SQ + SWA_WIDTH // PROJ_TN
_O_G = _O_SKV + 1
_N_G, _N_D, _N_S = N_GATE // PROJ_TN, DIFF_WIDTH // PROJ_TN, SWA_WIDTH // PROJ_TN


def _proj_src_tile(t):
    return jnp.where(t < _N_G, _O_G + t, jnp.where(t < _N_G + _N_D, _O_DK + (t - _N_G), _O_SKV))


def _proj_t_src_tile(t):
    return jnp.where(t < _N_D, _O_DQ + t,
                     jnp.where(t < 2 * _N_D, _O_DV + (t - _N_D),
                               jnp.where(t < 2 * _N_D + _N_S, _O_SQ + (t - 2 * _N_D), _O_SKV)))


PROJ_ROW_CHUNK = 256


def _proj_body(h_ref, w_ref, *refs):
    n = (len(refs) - 2) // 2
    cvt_in, o_ref, cvt_out, w_scr = refs[:n], refs[n], refs[n + 1:2 * n + 1], refs[2 * n + 1]

    def run(gated):
        w_scr[...] = w_ref[...].astype(BF16)
        _round_riders(cvt_in, cvt_out)
        for r in range(0, h_ref.shape[0], PROJ_ROW_CHUNK):
            rows = slice(r, r + PROJ_ROW_CHUNK)
            res = _dot(h_ref[rows, :], w_scr[...])
            if gated:
                res = 0.5 * jnp.tanh(0.5 * res) + 0.5
            o_ref[rows, :] = res.astype(BF16)

    gated = pl.program_id(1) < _N_G
    pl.when(gated)(lambda: run(True))
    pl.when(jnp.logical_not(gated))(lambda: run(False))


def _proj(h, w, convert=(), *, tm=2048):
    s, d = h.shape
    tn = PROJ_TN
    nt = PROJ_COLS // tn
    riders = _rider_specs(convert, (s // tm) * nt, lambda i, t: i * nt + t)
    return pl.pallas_call(
        _proj_body,
        grid=(s // tm, nt),
        in_specs=[pl.BlockSpec((tm, d), lambda i, t: (i, 0)),
                  pl.BlockSpec((d, tn), lambda i, t: (0, _proj_src_tile(t)))] + riders,
        out_specs=[pl.BlockSpec((tm, tn), lambda i, t: (i, t))] + riders,
        out_shape=[jax.ShapeDtypeStruct((s, PROJ_COLS), BF16)]
                  + [jax.ShapeDtypeStruct(m.shape, BF16) for m in convert],
        scratch_shapes=[pltpu.VMEM((d, tn), BF16)],
        compiler_params=_params("arbitrary", "arbitrary"),
        name="proj",
    )(h, w, *convert)


def _proj_t_body(h_ref, w_ref, o_ref, w_scr):
    w_scr[...] = w_ref[...].astype(BF16)
    for r in range(0, h_ref.shape[0], PROJ_ROW_CHUNK):
        res = _dot(h_ref[r:r + PROJ_ROW_CHUNK, :], w_scr[...])
        for k in range(PROJ_ROW_CHUNK // LANES):
            o_ref[r // LANES + k] = res[k * LANES:(k + 1) * LANES, :].T.astype(BF16)


def _proj_t(h, w, *, tm=2048):
    s, d = h.shape
    tn = PROJ_TN
    return pl.pallas_call(
        _proj_t_body,
        grid=(s // tm, PROJT_ROWS // tn),
        in_specs=[pl.BlockSpec((tm, d), lambda i, t: (i, 0)),
                  pl.BlockSpec((d, tn), lambda i, t: (0, _proj_t_src_tile(t)))],
        out_specs=pl.BlockSpec((tm // LANES, tn, LANES), lambda i, t: (i, t, 0)),
        out_shape=jax.ShapeDtypeStruct((s // LANES, PROJT_ROWS, LANES), BF16),
        scratch_shapes=[pltpu.VMEM((d, tn), BF16)],
        compiler_params=_params("arbitrary", "arbitrary"),
        name="proj_t",
    )(h, w)


DIFF_TQ = 512
DIFF_TK = 256
DIFF_ACC_ROWS = DIFF_V_DIM + 16
DIFF_HPS = 4
LOG2E = math.log2(math.e)


def _diff_body(slopes_ref, lam_ref, q_ref, k_ref, v_ref, e_ref, g_ref, o_ref, qaug_scr, *scr, lam_init):
    tq, tk, nh = DIFF_TQ, DIFF_TK, DIFF_TQ // DIFF_TK
    hp = pl.program_id(0)
    i = pl.program_id(1)
    z_scrs, zmax_scrs = scr[:DIFF_HPS], scr[DIFF_HPS:2 * DIFF_HPS]
    m_scr, acc_scr = scr[2 * DIFF_HPS:]
    slopes = [slopes_ref[hp * DIFF_HPS + hh] for hh in range(DIFF_HPS)]

    row = lax.broadcasted_iota(jnp.int32, (LANES, tq), 0)
    for hh in range(DIFF_HPS):
        q = jnp.concatenate([q_ref[s, hh * LANES:(hh + 1) * LANES, :] for s in range(tq // LANES)], axis=1)
        qs = (q.astype(F32) * (DIFF_HEAD_DIM ** -0.5 * LOG2E)).astype(BF16)
        zero = jnp.zeros_like(qs)
        s_f32 = jnp.where(row < 2, slopes[hh], jnp.where(row < 4, slopes[hh] * tk, 0.0))
        s_hi = s_f32.astype(BF16)
        s_lo = (s_f32 - s_hi.astype(F32)).astype(BF16)
        srow = jnp.where(row % 2 == 0, s_hi, s_lo)
        qaug_scr[2 * hh, 0:LANES, :] = jnp.where(row < DIFF_HEAD_DIM, qs, zero)
        qaug_scr[2 * hh + 1, 0:LANES, :] = jnp.where(row >= DIFF_HEAD_DIM, qs, zero)
        qaug_scr[2 * hh, LANES:2 * LANES, :] = srow
        qaug_scr[2 * hh + 1, LANES:2 * LANES, :] = srow
    m_scr[...] = jnp.full_like(m_scr, NEG_INF)
    acc_scr[...] = jnp.zeros_like(acc_scr)

    def qk(hh, j):
        for s in range(nh):
            rows = pl.ds(pl.multiple_of(j * tq + s * tk, tk), tk)
            kc = jnp.concatenate([k_ref[rows, hh * LANES:(hh + 1) * LANES],
                                  e_ref[s * tk:(s + 1) * tk, :]], axis=1)
            for c in range(2):
                z = _dot(kc, qaug_scr[2 * hh + c])
                z_scrs[hh][c * nh + s] = z
                zmax_scrs[hh][c * nh + s] = jnp.max(z, axis=0, keepdims=True)

    def softmax_pv(hh, j, diagonal):
        vt = jnp.concatenate([v_ref[(tq // LANES) * j + s, hh * LANES:(hh + 1) * LANES, :]
                              for s in range(tq // LANES)], axis=1)
        vt = jnp.concatenate([vt, jnp.ones((DIFF_ACC_ROWS - DIFF_V_DIM, tq), BF16)], axis=0)
        koff = (jnp.zeros((1, tq), jnp.int32) + j * tq).astype(F32) * slopes[hh]
        kr = lax.broadcasted_iota(jnp.int32, (tk, tq), 0)
        qc = lax.broadcasted_iota(jnp.int32, (tk, tq), 1)
        for c in range(2):
            zs = [z_scrs[hh][c * nh + s] for s in range(nh)]
            if diagonal:
                zs = [jnp.where(kr + s * tk <= qc, z, NEG_INF) for s, z in enumerate(zs)]
                zmaxs = [jnp.max(z, axis=0, keepdims=True) for z in zs]
            else:
                zmaxs = [zmax_scrs[hh][c * nh + s] for s in range(nh)]
            m_old = m_scr[2 * hh + c]
            m_tile = functools.reduce(jnp.maximum, zmaxs)
            m_new = jnp.maximum(m_old, m_tile + koff)
            alpha = jnp.exp2(m_old - m_new)
            shift = m_new - koff
            p = jnp.concatenate([jnp.exp2(z - shift).astype(BF16) for z in zs], axis=0)
            acc_scr[2 * hh + c] = alpha * acc_scr[2 * hh + c] + _dot(vt, p)
            m_scr[2 * hh + c] = m_new

    group_a = range(0, DIFF_HPS // 2)
    group_b = range(DIFF_HPS // 2, DIFF_HPS)
    for hh in group_a:
        qk(hh, 0)

    def body(j, carry):
        for hh in group_b:
            qk(hh, j)
        for hh in group_a:
            softmax_pv(hh, j, False)
        for hh in group_a:
            qk(hh, j + 1)
        for hh in group_b:
            softmax_pv(hh, j, False)
        return carry

    lax.fori_loop(0, i, body, 0)
    for hh in group_b:
        qk(hh, i)
    for hh in range(DIFF_HPS):
        softmax_pv(hh, i, True)

    lp = lam_ref[...]
    lam = (jnp.exp(jnp.sum(lp[0:1, :] * lp[1:2, :], axis=1, keepdims=True))
           - jnp.exp(jnp.sum(lp[2:3, :] * lp[3:4, :], axis=1, keepdims=True)) + lam_init)
    for hh in range(DIFF_HPS):
        a0, a1 = acc_scr[2 * hh], acc_scr[2 * hh + 1]
        o0 = a0[0:DIFF_V_DIM, :] * (1.0 / a0[DIFF_V_DIM:DIFF_V_DIM + 1, :])
        o1 = a1[0:DIFF_V_DIM, :] * (1.0 / a1[DIFF_V_DIM:DIFF_V_DIM + 1, :])
        o = o0 - lam * o1
        y = o * lax.rsqrt(jnp.mean(o * o, axis=0, keepdims=True) + NORM_EPS)
        o_ref[:, hh * DIFF_V_DIM:(hh + 1) * DIFF_V_DIM] = ((y.T * g_ref[...]) * (1.0 - lam_init)).astype(BF16)


def _diff_attn(slopes, lam_params, subln, proj, proj_t, *, lam_init):
    tq, tk, hps = DIFF_TQ, DIFF_TK, DIFF_HPS
    w = hps * LANES
    k_col0 = N_GATE // w
    v_row0 = DIFF_WIDTH // w
    r = np.arange(tq)[:, None]
    c = np.arange(LANES)[None, :]
    e = jnp.asarray(np.where(c < 2, r % tk, np.where(c < 4, r // tk, 0)), dtype=BF16)
    return pl.pallas_call(
        functools.partial(_diff_body, lam_init=lam_init),
        grid=(DIFF_HEADS // hps, SEQ // tq),
        in_specs=[
            pl.BlockSpec(memory_space=pltpu.SMEM),
            pl.BlockSpec((4, DIFF_HEAD_DIM), lambda h, i: (0, 0)),
            pl.BlockSpec((tq // LANES, w, LANES), lambda h, i: (i, h, 0)),
            pl.BlockSpec((SEQ, w), lambda h, i: (0, k_col0 + h)),
            pl.BlockSpec((N_SLABS, w, LANES), lambda h, i: (0, v_row0 + h, 0)),
            pl.BlockSpec((tq, LANES), lambda h, i: (0, 0)),
            pl.BlockSpec((1, DIFF_V_DIM), lambda h, i: (0, 0)),
        ],
        out_specs=pl.BlockSpec((tq, hps * DIFF_V_DIM), lambda h, i: (i, h)),
        out_shape=jax.ShapeDtypeStruct((SEQ, DIFF_WIDTH), BF16),
        scratch_shapes=[
            pltpu.VMEM((2 * hps, 2 * LANES, tq), BF16),
            *[pltpu.VMEM((2 * (tq // tk), tk, tq), F32)] * hps,
            *[pltpu.VMEM((2 * (tq // tk), 1, tq), F32)] * hps,
            pltpu.VMEM((2 * hps, 1, tq), F32),
            pltpu.VMEM((2 * hps, DIFF_ACC_ROWS, tq), F32),
        ],
        compiler_params=_params("arbitrary", "arbitrary"),
        name="diff_attn",
    )(slopes, lam_params, proj_t, proj, proj_t, e, subln)


SWA_TQ = 512
SWA_BLOCKS = SWA_TQ // WINDOW


def _swa_bias(slope, off):
    kb = lax.broadcasted_iota(jnp.int32, (2 * WINDOW, WINDOW), 0)
    qa = lax.broadcasted_iota(jnp.int32, (2 * WINDOW, WINDOW), 1)
    dist = off + qa - kb
    return jnp.where((dist >= 0) & (dist < WINDOW), -slope * dist.astype(F32), NEG_INF)


def _swa_body(slopes_ref, sinks_ref, q_ref, k_ref, v_ref, o_ref, qaug_scr, bias_scr):
    i = pl.program_id(0)

    @pl.when(i == 0)
    def _():
        qaug_scr[...] = jnp.zeros_like(qaug_scr)
        for hd in range(SWA_Q_HEADS):
            bias_scr[hd] = _swa_bias(slopes_ref[hd], WINDOW)

    def block(t, slab0, bias_of):
        kwin = k_ref[pl.ds(pl.multiple_of(slab0 * WINDOW, WINDOW), 2 * WINDOW), :]
        vt = jnp.concatenate([v_ref[slab0], v_ref[slab0 + 1]], axis=1)
        ones = jnp.ones((16, 2 * WINDOW), BF16)
        qs = q_ref[t]
        zs = []
        for kh in range(SWA_KV_HEADS):
            r0 = kh * SWA_HEAD_DIM
            for g in range(SWA_GROUP):
                hd = kh * SWA_GROUP + g
                qh = qs[hd * SWA_HEAD_DIM:(hd + 1) * SWA_HEAD_DIM, :].astype(F32) * (SWA_HEAD_DIM ** -0.5 * LOG2E)
                qaug_scr[kh, r0:r0 + SWA_HEAD_DIM, g * LANES:(g + 1) * LANES] = qh.astype(BF16)
            zs.append(_dot(kwin, qaug_scr[kh]))
        outs = []
        for kh in range(SWA_KV_HEADS):
            r0 = kh * SWA_HEAD_DIM
            z = zs[kh]
            vt_kh = jnp.concatenate([vt[r0:r0 + SWA_HEAD_DIM, :], ones], axis=0)
            row = []
            for g in range(SWA_GROUP):
                hd = kh * SWA_GROUP + g
                sink = sinks_ref[hd]
                sc = z[:, g * LANES:(g + 1) * LANES] + bias_of(hd)
                m = jnp.maximum(jnp.max(sc, axis=0, keepdims=True), sink)
                p = jnp.exp2(sc - m).astype(BF16)
                ot = _dot(vt_kh, p)
                denom = ot[SWA_HEAD_DIM:SWA_HEAD_DIM + 1, :] + jnp.exp2(sink - m)
                row.append(ot[0:SWA_HEAD_DIM, :] * (1.0 / denom))
            outs.append(row)
        r = pl.multiple_of(t * WINDOW, WINDOW)
        for pr in range(SWA_KV_HEADS // 2):
            for g in range(SWA_GROUP):
                x = jnp.concatenate([outs[2 * pr][g], outs[2 * pr + 1][g]], axis=0)
                c0 = (pr * SWA_GROUP + g) * LANES
                o_ref[pl.ds(r, WINDOW), c0:c0 + LANES] = x.T.astype(BF16)

    @pl.when(i == 0)
    def _():
        block(0, 0, lambda hd: _swa_bias(slopes_ref[hd], 0))

    def body(t, carry):
        block(t, i * SWA_BLOCKS + t - 1, lambda hd: bias_scr[hd])
        return carry

    lax.fori_loop(jnp.where(i == 0, 1, 0), SWA_BLOCKS, body, 0)


def _swa_attn(slopes, sinks, proj, proj_t):
    q_row_blk = (2 * DIFF_WIDTH) // SWA_WIDTH
    k_col_blk = (N_GATE + DIFF_WIDTH) // SWA_KV_WIDTH
    v_row_blk = (2 * DIFF_WIDTH + SWA_WIDTH) // SWA_KV_WIDTH + 1
    return pl.pallas_call(
        _swa_body,
        grid=(SEQ // SWA_TQ,),
        in_specs=[
            pl.BlockSpec(memory_space=pltpu.SMEM),
            pl.BlockSpec(memory_space=pltpu.SMEM),
            pl.BlockSpec((SWA_BLOCKS, SWA_WIDTH, LANES), lambda i: (i, q_row_blk, 0)),
            pl.BlockSpec((SEQ, SWA_KV_WIDTH), lambda i: (0, k_col_blk)),
            pl.BlockSpec((N_SLABS, SWA_KV_WIDTH, LANES), lambda i: (0, v_row_blk, 0)),
        ],
        out_specs=pl.BlockSpec((SWA_TQ, SWA_WIDTH), lambda i: (i, 0)),
        out_shape=jax.ShapeDtypeStruct((SEQ, SWA_WIDTH), BF16),
        scratch_shapes=[pltpu.VMEM((SWA_KV_HEADS, SWA_KV_WIDTH, SWA_GROUP * LANES), BF16),
                        pltpu.VMEM((SWA_Q_HEADS, 2 * WINDOW, WINDOW), F32)],
        compiler_params=_params("arbitrary"),
        name="swa_attn",
    )(slopes, sinks, proj_t, proj, proj_t)


def _mix_body(a_ref, b_ref, ga_ref, gb_ref, wa_ref, wb_ref, wo_ref, x_ref, gpost_ref, o_ref):
    j = pl.program_id(1)
    last = pl.num_programs(1) - 1
    tm = o_ref.shape[0]

    def step(chunk, first, final):
        for r in range(0, tm, chunk):
            rows = slice(r, r + chunk)
            ya = _dot(a_ref[rows, :], wa_ref[...])
            yb = _dot(b_ref[rows, :], wb_ref[...])
            mixed = ga_ref[rows, :].astype(F32) * ya + gb_ref[rows, :].astype(F32) * yb
            y = _dot(mixed.astype(BF16), wo_ref[...])
            if not first:
                y = o_ref[rows, :] + y
            if final:
                y = x_ref[rows, :] + _rms(y) * gpost_ref[...]
            o_ref[rows, :] = y

    pl.when(j == 0)(lambda: step(tm, True, False))
    pl.when((j > 0) & (j < last))(lambda: step(tm, False, False))
    pl.when(j == last)(lambda: step(FFN_EDGE_ROWS, False, True))


def _mix_out(a, b, proj, wa, wb, wo, x, gpost, *, tm=512, tn=1024):
    s, d = x.shape
    nj = d // tn
    return pl.pallas_call(
        _mix_body,
        grid=(s // tm, nj),
        in_specs=[
            pl.BlockSpec((tm, DIFF_WIDTH), lambda i, j: (i, 0)),
            pl.BlockSpec((tm, SWA_WIDTH), lambda i, j: (i, 0)),
            pl.BlockSpec((tm, tn), lambda i, j: (i, j)),
            pl.BlockSpec((tm, tn), lambda i, j: (i, nj + j)),
            pl.BlockSpec((DIFF_WIDTH, tn), lambda i, j: (0, j)),
            pl.BlockSpec((SWA_WIDTH, tn), lambda i, j: (0, j)),
            pl.BlockSpec((tn, d), lambda i, j: (j, 0)),
            pl.BlockSpec((tm, d), lambda i, j: (i, 0)),
            pl.BlockSpec((1, d), lambda i, j: (0, 0)),
        ],
        out_specs=pl.BlockSpec((tm, d), lambda i, j: (i, 0)),
        out_shape=jax.ShapeDtypeStruct((s, d), F32),
        compiler_params=_params("arbitrary", "arbitrary"),
        name="mix_out",
    )(a, b, proj, proj, wa, wb, wo, x, gpost)


def _alibi_slopes(n):
    return jnp.asarray(2.0 ** (-8.0 * np.arange(1, n + 1) / n), dtype=F32)


def kernel(x, ffn1_norm_pre, ffn1_w_in, ffn1_w_out, ffn1_norm_post, mix_norm_pre, w_in, diff_lambda, diff_subln, swa_sinks, w_branch_diff, w_branch_swa, w_out, mix_norm_post, ffn2_norm_pre, ffn2_w_in, ffn2_w_out, ffn2_norm_post):
    depth = ffn1_w_in.shape[0]
    xs = x.reshape(SEQ, D_MODEL)
    for l in range(depth):
        lam_init = 0.8 - 0.6 * math.exp(-0.3 * l)
        w = w_in[l]
        wbs = (w_branch_swa[l].reshape(2, 2, SWA_GROUP, SWA_HEAD_DIM, D_MODEL)
               .transpose(0, 2, 1, 3, 4).reshape(SWA_WIDTH, D_MODEL).astype(BF16))
        row = lambda v: v.reshape(1, -1)

        x1, h2, wbd, wo = _ffn(
            xs, row(ffn1_norm_pre[l]), ffn1_w_in[l].astype(BF16), ffn1_w_out[l].astype(BF16),
            row(ffn1_norm_post[l]), row(mix_norm_pre[l]), convert=(w_branch_diff[l], w_out[l]), tm=512)
        proj, w2_in, w2_out = _proj(h2, w, convert=(ffn2_w_in[l], ffn2_w_out[l]))
        proj_t = _proj_t(h2, w)
        a = _diff_attn(_alibi_slopes(DIFF_HEADS) * LOG2E, diff_lambda[l], row(diff_subln[l]), proj, proj_t,
                       lam_init=lam_init)
        b = _swa_attn(_alibi_slopes(SWA_Q_HEADS) * LOG2E, swa_sinks[l] * LOG2E, proj, proj_t)
        x2 = _mix_out(a, b, proj, wbd, wbs, wo, x1, row(mix_norm_post[l]))
        (xs,) = _ffn(x2, row(ffn2_norm_pre[l]), w2_in, w2_out, row(ffn2_norm_post[l]))
    return xs.reshape(x.shape)
```

```python
import functools
import math

import numpy as np
import jax
import jax.numpy as jnp
from jax import lax
from jax.experimental import pallas as pl
from jax.experimental.pallas import tpu as pltpu

D_MODEL = 2048
SEQ = 8192
DIFF_HEADS = 8
DIFF_HEAD_DIM = 64
DIFF_V_DIM = 128
DIFF_WIDTH = 1024
SWA_Q_HEADS = 16
SWA_KV_HEADS = 4
SWA_GROUP = 4
SWA_HEAD_DIM = 64
SWA_WIDTH = 1024
SWA_KV_WIDTH = 256
WINDOW = 128
D_FF = 5504
NORM_EPS = 1e-6
NEG_INF = -1e30

LANES = 128
MXU_DIM = 256
VMEM_LIMIT_BYTES = 60 * 1024 * 1024

N_GATE = 2 * D_MODEL
PROJ_COLS = N_GATE + DIFF_WIDTH + 2 * SWA_KV_WIDTH
PROJT_ROWS = DIFF_WIDTH + DIFF_WIDTH + SWA_WIDTH + 2 * SWA_KV_WIDTH
N_SLABS = SEQ // LANES

BF16 = jnp.bfloat16
F32 = jnp.float32


def _dot(a, b):
    return jnp.dot(a, b, preferred_element_type=F32)


def _rms(x):
    return x * lax.rsqrt(jnp.mean(x * x, axis=-1, keepdims=True) + NORM_EPS)


def _params(*sem, flags=None):
    return pltpu.CompilerParams(dimension_semantics=sem, vmem_limit_bytes=VMEM_LIMIT_BYTES, flags=flags)


def _rider_specs(mats, n_steps, step_of):
    specs = []
    for m in mats:
        rb = 16 * pl.cdiv(m.shape[0], 16 * n_steps)
        while m.shape[0] % rb:
            rb += 16
        specs.append(pl.BlockSpec((rb, m.shape[1]),
                                  lambda *ids, nb=m.shape[0] // rb: (jnp.minimum(step_of(*ids), nb - 1), 0)))
    return specs


def _round_riders(srcs, dsts):
    for src, dst in zip(srcs, dsts):
        dst[...] = src[...].astype(BF16)


FFN_ROW_CHUNK = 512
FFN_EDGE_ROWS = 256


def _ffn_body(*refs, emit_next, overlap, n_convert):
    refs = list(refs)
    x_ref, gpre_ref, wg_ref, wu_ref, wo_ref, gpost_ref = refs[:6]
    del refs[:6]
    gnext_ref = refs.pop(0) if emit_next else None
    cvt_in = [refs.pop(0) for _ in range(n_convert)]
    o_ref = refs.pop(0)
    hn_ref = refs.pop(0) if emit_next else None
    cvt_out = [refs.pop(0) for _ in range(n_convert)]
    (h_scr,) = refs
    j = pl.program_id(1)
    last = pl.num_programs(1) - 1
    tm = h_scr.shape[0]

    def step(chunk, first, final):
        _round_riders(cvt_in, cvt_out)
        for r in range(0, tm, chunk):
            rows = slice(r, r + chunk)
            if first:
                h = (_rms(x_ref[rows, :]) * gpre_ref[...]).astype(BF16)
                h_scr[rows, :] = h
            else:
                h = h_scr[rows, :]
            g = _dot(h, wg_ref[...])
            u = _dot(h, wu_ref[...])
            act = jax.nn.silu(g) * u
            if final and overlap:
                col = lax.broadcasted_iota(jnp.int32, act.shape, 1)
                act = jnp.where(col < overlap, 0.0, act)
            y = _dot(act.astype(BF16), wo_ref[...])
            if not first:
                y = o_ref[rows, :] + y
            if final:
                y = x_ref[rows, :] + 0.5 * (_rms(y) * gpost_ref[...])
                if emit_next:
                    hn_ref[rows, :] = (_rms(y) * gnext_ref[...]).astype(BF16)
            o_ref[rows, :] = y

    pl.when(j == 0)(lambda: step(FFN_EDGE_ROWS, True, False))
    pl.when((j > 0) & (j < last))(lambda: step(FFN_ROW_CHUNK, False, False))
    pl.when(j == last)(lambda: step(FFN_EDGE_ROWS, False, True))


def _ffn(x, gpre, w_in, w_out, gpost, gnext=None, convert=(), *, tm=1024, tf=512):
    emit_next = gnext is not None
    s, d = x.shape
    d_ff = w_out.shape[0]
    nj = pl.cdiv(d_ff, tf)
    overlap = nj * tf - d_ff
    assert tf % LANES == 0 and d_ff % LANES == 0
    start = lambda j, base=0: (jnp.minimum(j * (tf // LANES), (d_ff - tf) // LANES) + base // LANES) * LANES
    row = lambda i, j: (i, 0)
    fixed = lambda i, j: (0, 0)
    in_specs = [
        pl.BlockSpec((tm, d), row),
        pl.BlockSpec((1, d), fixed),
        pl.BlockSpec((pl.Element(d), pl.Element(tf)), lambda i, j: (0, start(j))),
        pl.BlockSpec((pl.Element(d), pl.Element(tf)), lambda i, j: (0, start(j, d_ff))),
        pl.BlockSpec((pl.Element(tf), pl.Element(d)), lambda i, j: (start(j), 0)),
        pl.BlockSpec((1, d), fixed),
    ]
    args = [x, gpre, w_in, w_in, w_out, gpost]
    out_shape = [jax.ShapeDtypeStruct((s, d), F32)]
    out_specs = [pl.BlockSpec((tm, d), row)]
    if emit_next:
        in_specs.append(pl.BlockSpec((1, d), fixed))
        args.append(gnext)
        out_shape.append(jax.ShapeDtypeStruct((s, d), BF16))
        out_specs.append(pl.BlockSpec((tm, d), row, pipeline_mode=pl.Buffered(1)))
    riders = _rider_specs(convert, (s // tm) * nj, lambda i, j: i * nj + j)
    in_specs += riders
    args += list(convert)
    out_specs += riders
    out_shape += [jax.ShapeDtypeStruct(m.shape, BF16) for m in convert]
    res = pl.pallas_call(
        functools.partial(_ffn_body, emit_next=emit_next, overlap=overlap, n_convert=len(convert)),
        grid=(s // tm, nj),
        in_specs=in_specs,
        out_specs=out_specs,
        out_shape=out_shape,
        scratch_shapes=[pltpu.VMEM((tm, d), BF16)],
        compiler_params=_params("arbitrary", "arbitrary"),
        name="ffn_next" if emit_next else "ffn",
    )(*args)
    return res


PROJ_TN = 2 * MXU_DIM
assert 2 * SWA_KV_WIDTH == PROJ_TN
_O_DQ, _O_DK, _O_DV, _O_SQ = (k * DIFF_WIDTH // PROJ_TN for k in range(4))
_O_SKV = _O_SQ + SWA_WIDTH // PROJ_TN
_O_G = _O_SKV + 1
_N_G, _N_D, _N_S = N_GATE // PROJ_TN, DIFF_WIDTH // PROJ_TN, SWA_WIDTH // PROJ_TN


def _proj_src_tile(t):
    return jnp.where(t < _N_G, _O_G + t, jnp.where(t < _N_G + _N_D, _O_DK + (t - _N_G), _O_SKV))


def _proj_t_src_tile(t):
    return jnp.where(t < _N_D, _O_DQ + t,
                     jnp.where(t < 2 * _N_D, _O_DV + (t - _N_D),
                               jnp.where(t < 2 * _N_D + _N_S, _O_SQ + (t - 2 * _N_D), _O_SKV)))


PROJ_ROW_CHUNK = 256


def _proj_body(h_ref, w_ref, *refs):
    n = (len(refs) - 2) // 2
    cvt_in, o_ref, cvt_out, w_scr = refs[:n], refs[n], refs[n + 1:2 * n + 1], refs[2 * n + 1]

    def run(gated):
        w_scr[...] = w_ref[...].astype(BF16)
        _round_riders(cvt_in, cvt_out)
        for r in range(0, h_ref.shape[0], PROJ_ROW_CHUNK):
            rows = slice(r, r + PROJ_ROW_CHUNK)
            res = _dot(h_ref[rows, :], w_scr[...])
            if gated:
                res = 0.5 * jnp.tanh(0.5 * res) + 0.5
            o_ref[rows, :] = res.astype(BF16)

    gated = pl.program_id(1) < _N_G
    pl.when(gated)(lambda: run(True))
    pl.when(jnp.logical_not(gated))(lambda: run(False))


def _proj(h, w, convert=(), *, tm=2048):
    s, d = h.shape
    tn = PROJ_TN
    nt = PROJ_COLS // tn
    riders = _rider_specs(convert, (s // tm) * nt, lambda i, t: i * nt + t)
    return pl.pallas_call(
        _proj_body,
        grid=(s // tm, nt),
        in_specs=[pl.BlockSpec((tm, d), lambda i, t: (i, 0)),
                  pl.BlockSpec((d, tn), lambda i, t: (0, _proj_src_tile(t)))] + riders,
        out_specs=[pl.BlockSpec((tm, tn), lambda i, t: (i, t))] + riders,
        out_shape=[jax.ShapeDtypeStruct((s, PROJ_COLS), BF16)]
                  + [jax.ShapeDtypeStruct(m.shape, BF16) for m in convert],
        scratch_shapes=[pltpu.VMEM((d, tn), BF16)],
        compiler_params=_params("arbitrary", "arbitrary"),
        name="proj",
    )(h, w, *convert)


def _proj_t_body(h_ref, w_ref, o_ref, w_scr):
    w_scr[...] = w_ref[...].astype(BF16)
    for r in range(0, h_ref.shape[0], PROJ_ROW_CHUNK):
        res = _dot(h_ref[r:r + PROJ_ROW_CHUNK, :], w_scr[...])
        for k in range(PROJ_ROW_CHUNK // LANES):
            o_ref[r // LANES + k] = res[k * LANES:(k + 1) * LANES, :].T.astype(BF16)


def _proj_t(h, w, *, tm=2048):
    s, d = h.shape
    tn = PROJ_TN
    return pl.pallas_call(
        _proj_t_body,
        grid=(s // tm, PROJT_ROWS // tn),
        in_specs=[pl.BlockSpec((tm, d), lambda i, t: (i, 0)),
                  pl.BlockSpec((d, tn), lambda i, t: (0, _proj_t_src_tile(t)))],
        out_specs=pl.BlockSpec((tm // LANES, tn, LANES), lambda i, t: (i, t, 0)),
        out_shape=jax.ShapeDtypeStruct((s // LANES, PROJT_ROWS, LANES), BF16),
        scratch_shapes=[pltpu.VMEM((d, tn), BF16)],
        compiler_params=_params("arbitrary", "arbitrary"),
        name="proj_t",
    )(h, w)


DIFF_TQ = 512
DIFF_TK = 256
DIFF_ACC_ROWS = DIFF_V_DIM + 16
DIFF_HPS = 4
LOG2E = math.log2(math.e)


def _diff_body(slopes_ref, lam_ref, q_ref, k_ref, v_ref, e_ref, g_ref, o_ref, qaug_scr, *scr, lam_init):
    tq, tk, nh = DIFF_TQ, DIFF_TK, DIFF_TQ // DIFF_TK
    hp = pl.program_id(0)
    i = pl.program_id(1)
    z_scrs, zmax_scrs = scr[:DIFF_HPS], scr[DIFF_HPS:2 * DIFF_HPS]
    m_scr, acc_scr = scr[2 * DIFF_HPS:]
    slopes = [slopes_ref[hp * DIFF_HPS + hh] for hh in range(DIFF_HPS)]

    row = lax.broadcasted_iota(jnp.int32, (LANES, tq), 0)
    for hh in range(DIFF_HPS):
        q = jnp.concatenate([q_ref[s, hh * LANES:(hh + 1) * LANES, :] for s in range(tq // LANES)], axis=1)
        qs = (q.astype(F32) * (DIFF_HEAD_DIM ** -0.5 * LOG2E)).astype(BF16)
        zero = jnp.zeros_like(qs)
        s_f32 = jnp.where(row < 2, slopes[hh], jnp.where(row < 4, slopes[hh] * tk, 0.0))
        s_hi = s_f32.astype(BF16)
        s_lo = (s_f32 - s_hi.astype(F32)).astype(BF16)
        srow = jnp.where(row % 2 == 0, s_hi, s_lo)
        qaug_scr[2 * hh, 0:LANES, :] = jnp.where(row < DIFF_HEAD_DIM, qs, zero)
        qaug_scr[2 * hh + 1, 0:LANES, :] = jnp.where(row >= DIFF_HEAD_DIM, qs, zero)
        qaug_scr[2 * hh, LANES:2 * LANES, :] = srow
        qaug_scr[2 * hh + 1, LANES:2 * LANES, :] = srow
    m_scr[...] = jnp.full_like(m_scr, NEG_INF)
    acc_scr[...] = jnp.zeros_like(acc_scr)

    def qk(hh, j):
        for s in range(nh):
            rows = pl.ds(pl.multiple_of(j * tq + s * tk, tk), tk)
            kc = jnp.concatenate([k_ref[rows, hh * LANES:(hh + 1) * LANES],
                                  e_ref[s * tk:(s + 1) * tk, :]], axis=1)
            for c in range(2):
                z = _dot(kc, qaug_scr[2 * hh + c])
                z_scrs[hh][c * nh + s] = z
                zmax_scrs[hh][c * nh + s] = jnp.max(z, axis=0, keepdims=True)

    def softmax_pv(hh, j, diagonal):
        vt = jnp.concatenate([v_ref[(tq // LANES) * j + s, hh * LANES:(hh + 1) * LANES, :]
                              for s in range(tq // LANES)], axis=1)
        vt = jnp.concatenate([vt, jnp.ones((DIFF_ACC_ROWS - DIFF_V_DIM, tq), BF16)], axis=0)
        koff = (jnp.zeros((1, tq), jnp.int32) + j * tq).astype(F32) * slopes[hh]
        kr = lax.broadcasted_iota(jnp.int32, (tk, tq), 0)
        qc = lax.broadcasted_iota(jnp.int32, (tk, tq), 1)
        for c in range(2):
            zs = [z_scrs[hh][c * nh + s] for s in range(nh)]
            if diagonal:
                zs = [jnp.where(kr + s * tk <= qc, z, NEG_INF) for s, z in enumerate(zs)]
                zmaxs = [jnp.max(z, axis=0, keepdims=True) for z in zs]
            else:
                zmaxs = [zmax_scrs[hh][c * nh + s] for s in range(nh)]
            m_old = m_scr[2 * hh + c]
            m_tile = functools.reduce(jnp.maximum, zmaxs)
            m_new = jnp.maximum(m_old, m_tile + koff)
            alpha = jnp.exp2(m_old - m_new)
            shift = m_new - koff
            p = jnp.concatenate([jnp.exp2(z - shift).astype(BF16) for z in zs], axis=0)
            acc_scr[2 * hh + c] = alpha * acc_scr[2 * hh + c] + _dot(vt, p)
            m_scr[2 * hh + c] = m_new

    group_a = range(0, DIFF_HPS // 2)
    group_b = range(DIFF_HPS // 2, DIFF_HPS)
    for hh in group_a:
        qk(hh, 0)

    def body(j, carry):
        for hh in group_b:
            qk(hh, j)
        for hh in group_a:
            softmax_pv(hh, j, False)
        for hh in group_a:
            qk(hh, j + 1)
        for hh in group_b:
            softmax_pv(hh, j, False)
        return carry

    lax.fori_loop(0, i, body, 0)
    for hh in group_b:
        qk(hh, i)
    for hh in range(DIFF_HPS):
        softmax_pv(hh, i, True)

    lp = lam_ref[...]
    lam = (jnp.exp(jnp.sum(lp[0:1, :] * lp[1:2, :], axis=1, keepdims=True))
           - jnp.exp(jnp.sum(lp[2:3, :] * lp[3:4, :], axis=1, keepdims=True)) + lam_init)
    for hh in range(DIFF_HPS):
        a0, a1 = acc_scr[2 * hh], acc_scr[2 * hh + 1]
        o0 = a0[0:DIFF_V_DIM, :] * (1.0 / a0[DIFF_V_DIM:DIFF_V_DIM + 1, :])
        o1 = a1[0:DIFF_V_DIM, :] * (1.0 / a1[DIFF_V_DIM:DIFF_V_DIM + 1, :])
        o = o0 - lam * o1
        y = o * lax.rsqrt(jnp.mean(o * o, axis=0, keepdims=True) + NORM_EPS)
        o_ref[:, hh * DIFF_V_DIM:(hh + 1) * DIFF_V_DIM] = ((y.T * g_ref[...]) * (1.0 - lam_init)).astype(BF16)


def _diff_attn(slopes, lam_params, subln, proj, proj_t, *, lam_init):
    tq, tk, hps = DIFF_TQ, DIFF_TK, DIFF_HPS
    w = hps * LANES
    k_col0 = N_GATE // w
    v_row0 = DIFF_WIDTH // w
    r = np.arange(tq)[:, None]
    c = np.arange(LANES)[None, :]
    e = jnp.asarray(np.where(c < 2, r % tk, np.where(c < 4, r // tk, 0)), dtype=BF16)
    return pl.pallas_call(
        functools.partial(_diff_body, lam_init=lam_init),
        grid=(DIFF_HEADS // hps, SEQ // tq),
        in_specs=[
            pl.BlockSpec(memory_space=pltpu.SMEM),
            pl.BlockSpec((4, DIFF_HEAD_DIM), lambda h, i: (0, 0)),
            pl.BlockSpec((tq // LANES, w, LANES), lambda h, i: (i, h, 0)),
            pl.BlockSpec((SEQ, w), lambda h, i: (0, k_col0 + h)),
            pl.BlockSpec((N_SLABS, w, LANES), lambda h, i: (0, v_row0 + h, 0)),
            pl.BlockSpec((tq, LANES), lambda h, i: (0, 0)),
            pl.BlockSpec((1, DIFF_V_DIM), lambda h, i: (0, 0)),
        ],
        out_specs=pl.BlockSpec((tq, hps * DIFF_V_DIM), lambda h, i: (i, h)),
        out_shape=jax.ShapeDtypeStruct((SEQ, DIFF_WIDTH), BF16),
        scratch_shapes=[
            pltpu.VMEM((2 * hps, 2 * LANES, tq), BF16),
            *[pltpu.VMEM((2 * (tq // tk), tk, tq), F32)] * hps,
            *[pltpu.VMEM((2 * (tq // tk), 1, tq), F32)] * hps,
            pltpu.VMEM((2 * hps, 1, tq), F32),
            pltpu.VMEM((2 * hps, DIFF_ACC_ROWS, tq), F32),
        ],
        compiler_params=_params("arbitrary", "arbitrary"),
        name="diff_attn",
    )(slopes, lam_params, proj_t, proj, proj_t, e, subln)


SWA_TQ = 512
SWA_BLOCKS = SWA_TQ // WINDOW


def _swa_bias(slope, off):
    kb = lax.broadcasted_iota(jnp.int32, (2 * WINDOW, WINDOW), 0)
    qa = lax.broadcasted_iota(jnp.int32, (2 * WINDOW, WINDOW), 1)
    dist = off + qa - kb
    return jnp.where((dist >= 0) & (dist < WINDOW), -slope * dist.astype(F32), NEG_INF)


def _swa_body(slopes_ref, sinks_ref, q_ref, k_ref, v_ref, o_ref, qaug_scr, bias_scr):
    i = pl.program_id(0)

    @pl.when(i == 0)
    def _():
        qaug_scr[...] = jnp.zeros_like(qaug_scr)
        for hd in range(SWA_Q_HEADS):
            bias_scr[hd] = _swa_bias(slopes_ref[hd], WINDOW)

    def block(t, slab0, bias_of):
        kwin = k_ref[pl.ds(pl.multiple_of(slab0 * WINDOW, WINDOW), 2 * WINDOW), :]
        vt = jnp.concatenate([v_ref[slab0], v_ref[slab0 + 1]], axis=1)
        ones = jnp.ones((16, 2 * WINDOW), BF16)
        qs = q_ref[t]
        zs = []
        for kh in range(SWA_KV_HEADS):
            r0 = kh * SWA_HEAD_DIM
            for g in range(SWA_GROUP):
                hd = kh * SWA_GROUP + g
                qh = qs[hd * SWA_HEAD_DIM:(hd + 1) * SWA_HEAD_DIM, :].astype(F32) * (SWA_HEAD_DIM ** -0.5 * LOG2E)
                qaug_scr[kh, r0:r0 + SWA_HEAD_DIM, g * LANES:(g + 1) * LANES] = qh.astype(BF16)
            zs.append(_dot(kwin, qaug_scr[kh]))
        outs = []
        for kh in range(SWA_KV_HEADS):
            r0 = kh * SWA_HEAD_DIM
            z = zs[kh]
            vt_kh = jnp.concatenate([vt[r0:r0 + SWA_HEAD_DIM, :], ones], axis=0)
            row = []
            for g in range(SWA_GROUP):
                hd = kh * SWA_GROUP + g
                sink = sinks_ref[hd]
                sc = z[:, g * LANES:(g + 1) * LANES] + bias_of(hd)
                m = jnp.maximum(jnp.max(sc, axis=0, keepdims=True), sink)
                p = jnp.exp2(sc - m).astype(BF16)
                ot = _dot(vt_kh, p)
                denom = ot[SWA_HEAD_DIM:SWA_HEAD_DIM + 1, :] + jnp.exp2(sink - m)
                row.append(ot[0:SWA_HEAD_DIM, :] * (1.0 / denom))
            outs.append(row)
        r = pl.multiple_of(t * WINDOW, WINDOW)
        for pr in range(SWA_KV_HEADS // 2):
            for g in range(SWA_GROUP):
                x = jnp.concatenate([outs[2 * pr][g], outs[2 * pr + 1][g]], axis=0)
                c0 = (pr * SWA_GROUP + g) * LANES
                o_ref[pl.ds(r, WINDOW), c0:c0 + LANES] = x.T.astype(BF16)

    @pl.when(i == 0)
    def _():
        block(0, 0, lambda hd: _swa_bias(slopes_ref[hd], 0))

    def body(t, carry):
        block(t, i * SWA_BLOCKS + t - 1, lambda hd: bias_scr[hd])
        return carry

    lax.fori_loop(jnp.where(i == 0, 1, 0), SWA_BLOCKS, body, 0)


def _swa_attn(slopes, sinks, proj, proj_t):
    q_row_blk = (2 * DIFF_WIDTH) // SWA_WIDTH
    k_col_blk = (N_GATE + DIFF_WIDTH) // SWA_KV_WIDTH
    v_row_blk = (2 * DIFF_WIDTH + SWA_WIDTH) // SWA_KV_WIDTH + 1
    return pl.pallas_call(
        _swa_body,
        grid=(SEQ // SWA_TQ,),
        in_specs=[
            pl.BlockSpec(memory_space=pltpu.SMEM),
            pl.BlockSpec(memory_space=pltpu.SMEM),
            pl.BlockSpec((SWA_BLOCKS, SWA_WIDTH, LANES), lambda i: (i, q_row_blk, 0)),
            pl.BlockSpec((SEQ, SWA_KV_WIDTH), lambda i: (0, k_col_blk)),
            pl.BlockSpec((N_SLABS, SWA_KV_WIDTH, LANES), lambda i: (0, v_row_blk, 0)),
        ],
        out_specs=pl.BlockSpec((SWA_TQ, SWA_WIDTH), lambda i: (i, 0)),
        out_shape=jax.ShapeDtypeStruct((SEQ, SWA_WIDTH), BF16),
        scratch_shapes=[pltpu.VMEM((SWA_KV_HEADS, SWA_KV_WIDTH, SWA_GROUP * LANES), BF16),
                        pltpu.VMEM((SWA_Q_HEADS, 2 * WINDOW, WINDOW), F32)],
        compiler_params=_params("arbitrary"),
        name="swa_attn",
    )(slopes, sinks, proj_t, proj, proj_t)


def _mix_body(a_ref, b_ref, ga_ref, gb_ref, wa_ref, wb_ref, wo_ref, x_ref, gpost_ref, o_ref):
    j = pl.program_id(1)
    last = pl.num_programs(1) - 1
    tm = o_ref.shape[0]

    def step(chunk, first, final):
        for r in range(0, tm, chunk):
            rows = slice(r, r + chunk)
            ya = _dot(a_ref[rows, :], wa_ref[...])
            yb = _dot(b_ref[rows, :], wb_ref[...])
            mixed = ga_ref[rows, :].astype(F32) * ya + gb_ref[rows, :].astype(F32) * yb
            y = _dot(mixed.astype(BF16), wo_ref[...])
            if not first:
                y = o_ref[rows, :] + y
            if final:
                y = x_ref[rows, :] + _rms(y) * gpost_ref[...]
            o_ref[rows, :] = y

    pl.when(j == 0)(lambda: step(tm, True, False))
    pl.when((j > 0) & (j < last))(lambda: step(tm, False, False))
    pl.when(j == last)(lambda: step(FFN_EDGE_ROWS, False, True))


def _mix_out(a, b, proj, wa, wb, wo, x, gpost, *, tm=512, tn=1024):
    s, d = x.shape
    nj = d // tn
    return pl.pallas_call(
        _mix_body,
        grid=(s // tm, nj),
        in_specs=[
            pl.BlockSpec((tm, DIFF_WIDTH), lambda i, j: (i, 0)),
            pl.BlockSpec((tm, SWA_WIDTH), lambda i, j: (i, 0)),
            pl.BlockSpec((tm, tn), lambda i, j: (i, j)),
            pl.BlockSpec((tm, tn), lambda i, j: (i, nj + j)),
            pl.BlockSpec((DIFF_WIDTH, tn), lambda i, j: (0, j)),
            pl.BlockSpec((SWA_WIDTH, tn), lambda i, j: (0, j)),
            pl.BlockSpec((tn, d), lambda i, j: (j, 0)),
            pl.BlockSpec((tm, d), lambda i, j: (i, 0)),
            pl.BlockSpec((1, d), lambda i, j: (0, 0)),
        ],
        out_specs=pl.BlockSpec((tm, d), lambda i, j: (i, 0)),
        out_shape=jax.ShapeDtypeStruct((s, d), F32),
        compiler_params=_params("arbitrary", "arbitrary"),
        name="mix_out",
    )(a, b, proj, proj, wa, wb, wo, x, gpost)


def _alibi_slopes(n):
    return jnp.asarray(2.0 ** (-8.0 * np.arange(1, n + 1) / n), dtype=F32)


def kernel(x, ffn1_norm_pre, ffn1_w_in, ffn1_w_out, ffn1_norm_post, mix_norm_pre, w_in, diff_lambda, diff_subln, swa_sinks, w_branch_diff, w_branch_swa, w_out, mix_norm_post, ffn2_norm_pre, ffn2_w_in, ffn2_w_out, ffn2_norm_post):
    depth = ffn1_w_in.shape[0]
    xs = x.reshape(SEQ, D_MODEL)
    for l in range(depth):
        lam_init = 0.8 - 0.6 * math.exp(-0.3 * l)
        w = w_in[l]
        wbs = (w_branch_swa[l].reshape(2, 2, SWA_GROUP, SWA_HEAD_DIM, D_MODEL)
               .transpose(0, 2, 1, 3, 4).reshape(SWA_WIDTH, D_MODEL).astype(BF16))
        row = lambda v: v.reshape(1, -1)

        x1, h2, wbd, wo = _ffn(
            xs, row(ffn1_norm_pre[l]), ffn1_w_in[l].astype(BF16), ffn1_w_out[l].astype(BF16),
            row(ffn1_norm_post[l]), row(mix_norm_pre[l]), convert=(w_branch_diff[l], w_out[l]))
        proj, w2_in, w2_out = _proj(h2, w, convert=(ffn2_w_in[l], ffn2_w_out[l]))
        proj_t = _proj_t(h2, w)
        a = _diff_attn(_alibi_slopes(DIFF_HEADS) * LOG2E, diff_lambda[l], row(diff_subln[l]), proj, proj_t,
                       lam_init=lam_init)
        b = _swa_attn(_alibi_slopes(SWA_Q_HEADS) * LOG2E, swa_sinks[l] * LOG2E, proj, proj_t)
        x2 = _mix_out(a, b, proj, wbd, wbs, wo, x1, row(mix_norm_post[l]))
        (xs,) = _ffn(x2, row(ffn2_norm_pre[l]), w2_in, w2_out, row(ffn2_norm_post[l]))
    return xs.reshape(x.shape)
```

```python
import functools
import math

import numpy as np
import jax
import jax.numpy as jnp
from jax import lax
from jax.experimental import pallas as pl
from jax.experimental.pallas import tpu as pltpu

D_MODEL = 2048
SEQ = 8192
DIFF_HEADS = 8
DIFF_HEAD_DIM = 64
DIFF_V_DIM = 128
DIFF_WIDTH = 1024
SWA_Q_HEADS = 16
SWA_KV_HEADS = 4
SWA_GROUP = 4
SWA_HEAD_DIM = 64
SWA_WIDTH = 1024
SWA_KV_WIDTH = 256
WINDOW = 128
D_FF = 5504
NORM_EPS = 1e-6
NEG_INF = -1e30

LANES = 128
MXU_DIM = 256
VMEM_LIMIT_BYTES = 60 * 1024 * 1024

N_GATE = 2 * D_MODEL
PROJ_COLS = N_GATE + DIFF_WIDTH + 2 * SWA_KV_WIDTH
PROJT_ROWS = DIFF_WIDTH + DIFF_WIDTH + SWA_WIDTH + 2 * SWA_KV_WIDTH
N_SLABS = SEQ // LANES

BF16 = jnp.bfloat16
F32 = jnp.float32


def _dot(a, b):
    return jnp.dot(a, b, preferred_element_type=F32)


def _rms(x):
    return x * lax.rsqrt(jnp.mean(x * x, axis=-1, keepdims=True) + NORM_EPS)


def _params(*sem, flags=None):
    return pltpu.CompilerParams(dimension_semantics=sem, vmem_limit_bytes=VMEM_LIMIT_BYTES, flags=flags)


def _rider_specs(mats, n_steps, step_of):
    specs = []
    for m in mats:
        rb = 16 * pl.cdiv(m.shape[0], 16 * n_steps)
        while m.shape[0] % rb:
            rb += 16
        specs.append(pl.BlockSpec((rb, m.shape[1]),
                                  lambda *ids, nb=m.shape[0] // rb: (jnp.minimum(step_of(*ids), nb - 1), 0)))
    return specs


def _round_riders(srcs, dsts):
    for src, dst in zip(srcs, dsts):
        dst[...] = src[...].astype(BF16)


FFN_ROW_CHUNK = 512
FFN_EDGE_ROWS = 256


def _ffn_body(*refs, emit_next, overlap, n_convert):
    refs = list(refs)
    x_ref, gpre_ref, wg_ref, wu_ref, wo_ref, gpost_ref = refs[:6]
    del refs[:6]
    gnext_ref = refs.pop(0) if emit_next else None
    cvt_in = [refs.pop(0) for _ in range(n_convert)]
    o_ref = refs.pop(0)
    hn_ref = refs.pop(0) if emit_next else None
    cvt_out = [refs.pop(0) for _ in range(n_convert)]
    (h_scr,) = refs
    j = pl.program_id(1)
    last = pl.num_programs(1) - 1
    tm = h_scr.shape[0]

    def step(chunk, first, final):
        _round_riders(cvt_in, cvt_out)
        for r in range(0, tm, chunk):
            rows = slice(r, r + chunk)
            if first:
                h = (_rms(x_ref[rows, :]) * gpre_ref[...]).astype(BF16)
                h_scr[rows, :] = h
            else:
                h = h_scr[rows, :]
            g = _dot(h, wg_ref[...])
            u = _dot(h, wu_ref[...])
            act = jax.nn.silu(g) * u
            if final and overlap:
                col = lax.broadcasted_iota(jnp.int32, act.shape, 1)
                act = jnp.where(col < overlap, 0.0, act)
            y = _dot(act.astype(BF16), wo_ref[...])
            if not first:
                y = o_ref[rows, :] + y
            if final:
                y = x_ref[rows, :] + 0.5 * (_rms(y) * gpost_ref[...])
                if emit_next:
                    hn_ref[rows, :] = (_rms(y) * gnext_ref[...]).astype(BF16)
            o_ref[rows, :] = y

    pl.when(j == 0)(lambda: step(FFN_EDGE_ROWS, True, False))
    pl.when((j > 0) & (j < last))(lambda: step(FFN_ROW_CHUNK, False, False))
    pl.when(j == last)(lambda: step(FFN_EDGE_ROWS, False, True))


def _ffn(x, gpre, w_in, w_out, gpost, gnext=None, convert=(), *, tm=1024, tf=512):
    emit_next = gnext is not None
    s, d = x.shape
    d_ff = w_out.shape[0]
    nj = pl.cdiv(d_ff, tf)
    overlap = nj * tf - d_ff
    assert tf % LANES == 0 and d_ff % LANES == 0
    start = lambda j, base=0: (jnp.minimum(j * (tf // LANES), (d_ff - tf) // LANES) + base // LANES) * LANES
    row = lambda i, j: (i, 0)
    fixed = lambda i, j: (0, 0)
    in_specs = [
        pl.BlockSpec((tm, d), row),
        pl.BlockSpec((1, d), fixed),
        pl.BlockSpec((pl.Element(d), pl.Element(tf)), lambda i, j: (0, start(j))),
        pl.BlockSpec((pl.Element(d), pl.Element(tf)), lambda i, j: (0, start(j, d_ff))),
        pl.BlockSpec((pl.Element(tf), pl.Element(d)), lambda i, j: (start(j), 0)),
        pl.BlockSpec((1, d), fixed),
    ]
    args = [x, gpre, w_in, w_in, w_out, gpost]
    out_shape = [jax.ShapeDtypeStruct((s, d), F32)]
    out_specs = [pl.BlockSpec((tm, d), row)]
    if emit_next:
        in_specs.append(pl.BlockSpec((1, d), fixed))
        args.append(gnext)
        out_shape.append(jax.ShapeDtypeStruct((s, d), BF16))
        out_specs.append(pl.BlockSpec((tm, d), row, pipeline_mode=pl.Buffered(1)))
    riders = _rider_specs(convert, (s // tm) * nj, lambda i, j: i * nj + j)
    in_specs += riders
    args += list(convert)
    out_specs += riders
    out_shape += [jax.ShapeDtypeStruct(m.shape, BF16) for m in convert]
    res = pl.pallas_call(
        functools.partial(_ffn_body, emit_next=emit_next, overlap=overlap, n_convert=len(convert)),
        grid=(s // tm, nj),
        in_specs=in_specs,
        out_specs=out_specs,
        out_shape=out_shape,
        scratch_shapes=[pltpu.VMEM((tm, d), BF16)],
        compiler_params=_params("arbitrary", "arbitrary"),
        name="ffn_next" if emit_next else "ffn",
    )(*args)
    return res


PROJ_TN = 2 * MXU_DIM
assert 2 * SWA_KV_WIDTH == PROJ_TN
_O_DQ, _O_DK, _O_DV, _O_SQ = (k * DIFF_WIDTH // PROJ_TN for k in range(4))
_O_SKV = _O_SQ + SWA_WIDTH // PROJ_TN
_O_G = _O_SKV + 1
_N_G, _N_D, _N_S = N_GATE // PROJ_TN, DIFF_WIDTH // PROJ_TN, SWA_WIDTH // PROJ_TN


def _proj_src_tile(t):
    return jnp.where(t < _N_G, _O_G + t, jnp.where(t < _N_G + _N_D, _O_DK + (t - _N_G), _O_SKV))


def _proj_t_src_tile(t):
    return jnp.where(t < _N_D, _O_DQ + t,
                     jnp.where(t < 2 * _N_D, _O_DV + (t - _N_D),
                               jnp.where(t < 2 * _N_D + _N_S, _O_SQ + (t - 2 * _N_D), _O_SKV)))


PROJ_ROW_CHUNK = 256


def _proj_body(h_ref, w_ref, *refs):
    n = (len(refs) - 2) // 2
    cvt_in, o_ref, cvt_out, w_scr = refs[:n], refs[n], refs[n + 1:2 * n + 1], refs[2 * n + 1]

    def run(gated):
        w_scr[...] = w_ref[...].astype(BF16)
        _round_riders(cvt_in, cvt_out)
        for r in range(0, h_ref.shape[0], PROJ_ROW_CHUNK):
            rows = slice(r, r + PROJ_ROW_CHUNK)
            res = _dot(h_ref[rows, :], w_scr[...])
            if gated:
                res = 0.5 * jnp.tanh(0.5 * res) + 0.5
            o_ref[rows, :] = res.astype(BF16)

    gated = pl.program_id(1) < _N_G
    pl.when(gated)(lambda: run(True))
    pl.when(jnp.logical_not(gated))(lambda: run(False))


def _proj(h, w, convert=(), *, tm=2048):
    s, d = h.shape
    tn = PROJ_TN
    nt = PROJ_COLS // tn
    riders = _rider_specs(convert, (s // tm) * nt, lambda i, t: i * nt + t)
    return pl.pallas_call(
        _proj_body,
        grid=(s // tm, nt),
        in_specs=[pl.BlockSpec((tm, d), lambda i, t: (i, 0)),
                  pl.BlockSpec((d, tn), lambda i, t: (0, _proj_src_tile(t)))] + riders,
        out_specs=[pl.BlockSpec((tm, tn), lambda i, t: (i, t))] + riders,
        out_shape=[jax.ShapeDtypeStruct((s, PROJ_COLS), BF16)]
                  + [jax.ShapeDtypeStruct(m.shape, BF16) for m in convert],
        scratch_shapes=[pltpu.VMEM((d, tn), BF16)],
        compiler_params=_params("arbitrary", "arbitrary"),
        name="proj",
    )(h, w, *convert)


def _proj_t_body(h_ref, w_ref, *refs):
    n = (len(refs) - 2) // 2
    cvt_in, o_ref, cvt_out, w_scr = refs[:n], refs[n], refs[n + 1:2 * n + 1], refs[2 * n + 1]
    w_scr[...] = w_ref[...].astype(BF16)
    _round_riders(cvt_in, cvt_out)
    for r in range(0, h_ref.shape[0], PROJ_ROW_CHUNK):
        res = _dot(h_ref[r:r + PROJ_ROW_CHUNK, :], w_scr[...])
        for k in range(PROJ_ROW_CHUNK // LANES):
            o_ref[r // LANES + k] = res[k * LANES:(k + 1) * LANES, :].T.astype(BF16)


def _proj_t(h, w, convert=(), *, tm=2048):
    s, d = h.shape
    tn = PROJ_TN
    nt = PROJT_ROWS // tn
    riders = _rider_specs(convert, (s // tm) * nt, lambda i, t: i * nt + t)
    return pl.pallas_call(
        _proj_t_body,
        grid=(s // tm, nt),
        in_specs=[pl.BlockSpec((tm, d), lambda i, t: (i, 0)),
                  pl.BlockSpec((d, tn), lambda i, t: (0, _proj_t_src_tile(t)))] + riders,
        out_specs=[pl.BlockSpec((tm // LANES, tn, LANES), lambda i, t: (i, t, 0))] + riders,
        out_shape=[jax.ShapeDtypeStruct((s // LANES, PROJT_ROWS, LANES), BF16)]
                  + [jax.ShapeDtypeStruct(m.shape, BF16) for m in convert],
        scratch_shapes=[pltpu.VMEM((d, tn), BF16)],
        compiler_params=_params("arbitrary", "arbitrary"),
        name="proj_t",
    )(h, w, *convert)


DIFF_TQ = 512
DIFF_TK = 256
DIFF_ACC_ROWS = DIFF_V_DIM + 16
DIFF_HPS = 4
LOG2E = math.log2(math.e)


def _diff_body(slopes_ref, lam_ref, q_ref, k_ref, v_ref, e_ref, g_ref, o_ref, qaug_scr, *scr, lam_init):
    tq, tk, nh = DIFF_TQ, DIFF_TK, DIFF_TQ // DIFF_TK
    hp = pl.program_id(0)
    i = pl.program_id(1)
    z_scrs, zmax_scrs = scr[:DIFF_HPS], scr[DIFF_HPS:2 * DIFF_HPS]
    m_scr, acc_scr = scr[2 * DIFF_HPS:]
    slopes = [slopes_ref[hp * DIFF_HPS + hh] for hh in range(DIFF_HPS)]

    row = lax.broadcasted_iota(jnp.int32, (LANES, tq), 0)

    def build_queries(hh):
        q = jnp.concatenate([q_ref[s, hh * LANES:(hh + 1) * LANES, :] for s in range(tq // LANES)], axis=1)
        qs = (q.astype(F32) * (DIFF_HEAD_DIM ** -0.5 * LOG2E)).astype(BF16)
        zero = jnp.zeros_like(qs)
        s_f32 = jnp.where(row < 2, slopes[hh], jnp.where(row < 4, slopes[hh] * tk, 0.0))
        s_hi = s_f32.astype(BF16)
        s_lo = (s_f32 - s_hi.astype(F32)).astype(BF16)
        srow = jnp.where(row % 2 == 0, s_hi, s_lo)
        qaug_scr[2 * hh, 0:LANES, :] = jnp.where(row < DIFF_HEAD_DIM, qs, zero)
        qaug_scr[2 * hh + 1, 0:LANES, :] = jnp.where(row >= DIFF_HEAD_DIM, qs, zero)
        qaug_scr[2 * hh, LANES:2 * LANES, :] = srow
        qaug_scr[2 * hh + 1, LANES:2 * LANES, :] = srow

    def qk(hh, j):
        for s in range(nh):
            rows = pl.ds(pl.multiple_of(j * tq + s * tk, tk), tk)
            kc = jnp.concatenate([k_ref[rows, hh * LANES:(hh + 1) * LANES],
                                  e_ref[s * tk:(s + 1) * tk, :]], axis=1)
            for c in range(2):
                z = _dot(kc, qaug_scr[2 * hh + c])
                z_scrs[hh][c * nh + s] = z
                zmax_scrs[hh][c * nh + s] = jnp.max(z, axis=0, keepdims=True)

    def softmax_pv(hh, j, diagonal):
        vt = jnp.concatenate([v_ref[(tq // LANES) * j + s, hh * LANES:(hh + 1) * LANES, :]
                              for s in range(tq // LANES)], axis=1)
        vt = jnp.concatenate([vt, jnp.ones((DIFF_ACC_ROWS - DIFF_V_DIM, tq), BF16)], axis=0)
        koff = (jnp.zeros((1, tq), jnp.int32) + j * tq).astype(F32) * slopes[hh]
        kr = lax.broadcasted_iota(jnp.int32, (tk, tq), 0)
        qc = lax.broadcasted_iota(jnp.int32, (tk, tq), 1)
        for c in range(2):
            zs = [z_scrs[hh][c * nh + s] for s in range(nh)]
            if diagonal:
                zs = [jnp.where(kr + s * tk <= qc, z, NEG_INF) for s, z in enumerate(zs)]
                zmaxs = [jnp.max(z, axis=0, keepdims=True) for z in zs]
            else:
                zmaxs = [zmax_scrs[hh][c * nh + s] for s in range(nh)]
            m_old = m_scr[2 * hh + c]
            m_tile = functools.reduce(jnp.maximum, zmaxs)
            m_new = jnp.maximum(m_old, m_tile + koff)
            alpha = jnp.exp2(m_old - m_new)
            shift = m_new - koff
            p = jnp.concatenate([jnp.exp2(z - shift).astype(BF16) for z in zs], axis=0)
            acc_scr[2 * hh + c] = alpha * acc_scr[2 * hh + c] + _dot(vt, p)
            m_scr[2 * hh + c] = m_new

    group_a = range(0, DIFF_HPS // 2)
    group_b = range(DIFF_HPS // 2, DIFF_HPS)
    for hh in group_a:
        build_queries(hh)
    for hh in group_a:
        qk(hh, 0)
    for hh in group_b:
        build_queries(hh)
    m_scr[...] = jnp.full_like(m_scr, NEG_INF)
    acc_scr[...] = jnp.zeros_like(acc_scr)

    def body(j, carry):
        for hh in group_b:
            qk(hh, j)
        for hh in group_a:
            softmax_pv(hh, j, False)
        for hh in group_a:
            qk(hh, j + 1)
        for hh in group_b:
            softmax_pv(hh, j, False)
        return carry

    lax.fori_loop(0, i, body, 0)
    for hh in group_b:
        qk(hh, i)
    for hh in range(DIFF_HPS):
        softmax_pv(hh, i, True)

    lp = lam_ref[...]
    lam = (jnp.exp(jnp.sum(lp[0:1, :] * lp[1:2, :], axis=1, keepdims=True))
           - jnp.exp(jnp.sum(lp[2:3, :] * lp[3:4, :], axis=1, keepdims=True)) + lam_init)
    for hh in range(DIFF_HPS):
        a0, a1 = acc_scr[2 * hh], acc_scr[2 * hh + 1]
        o0 = a0[0:DIFF_V_DIM, :] * (1.0 / a0[DIFF_V_DIM:DIFF_V_DIM + 1, :])
        o1 = a1[0:DIFF_V_DIM, :] * (1.0 / a1[DIFF_V_DIM:DIFF_V_DIM + 1, :])
        o = o0 - lam * o1
        y = o * lax.rsqrt(jnp.mean(o * o, axis=0, keepdims=True) + NORM_EPS)
        o_ref[:, hh * DIFF_V_DIM:(hh + 1) * DIFF_V_DIM] = ((y.T * g_ref[...]) * (1.0 - lam_init)).astype(BF16)


def _diff_attn(slopes, lam_params, subln, proj, proj_t, *, lam_init):
    tq, tk, hps = DIFF_TQ, DIFF_TK, DIFF_HPS
    w = hps * LANES
    k_col0 = N_GATE // w
    v_row0 = DIFF_WIDTH // w
    r = np.arange(tq)[:, None]
    c = np.arange(LANES)[None, :]
    e = jnp.asarray(np.where(c < 2, r % tk, np.where(c < 4, r // tk, 0)), dtype=BF16)
    return pl.pallas_call(
        functools.partial(_diff_body, lam_init=lam_init),
        grid=(DIFF_HEADS // hps, SEQ // tq),
        in_specs=[
            pl.BlockSpec(memory_space=pltpu.SMEM),
            pl.BlockSpec((4, DIFF_HEAD_DIM), lambda h, i: (0, 0)),
            pl.BlockSpec((tq // LANES, w, LANES), lambda h, i: (i, h, 0)),
            pl.BlockSpec((SEQ, w), lambda h, i: (0, k_col0 + h)),
            pl.BlockSpec((N_SLABS, w, LANES), lambda h, i: (0, v_row0 + h, 0)),
            pl.BlockSpec((tq, LANES), lambda h, i: (0, 0)),
            pl.BlockSpec((1, DIFF_V_DIM), lambda h, i: (0, 0)),
        ],
        out_specs=pl.BlockSpec((tq, hps * DIFF_V_DIM), lambda h, i: (i, h)),
        out_shape=jax.ShapeDtypeStruct((SEQ, DIFF_WIDTH), BF16),
        scratch_shapes=[
            pltpu.VMEM((2 * hps, 2 * LANES, tq), BF16),
            *[pltpu.VMEM((2 * (tq // tk), tk, tq), F32)] * hps,
            *[pltpu.VMEM((2 * (tq // tk), 1, tq), F32)] * hps,
            pltpu.VMEM((2 * hps, 1, tq), F32),
            pltpu.VMEM((2 * hps, DIFF_ACC_ROWS, tq), F32),
        ],
        compiler_params=_params("arbitrary", "arbitrary"),
        name="diff_attn",
    )(slopes, lam_params, proj_t, proj, proj_t, e, subln)


SWA_TQ = 512
SWA_BLOCKS = SWA_TQ // WINDOW


def _swa_bias(slope, off):
    kb = lax.broadcasted_iota(jnp.int32, (2 * WINDOW, WINDOW), 0)
    qa = lax.broadcasted_iota(jnp.int32, (2 * WINDOW, WINDOW), 1)
    dist = off + qa - kb
    return jnp.where((dist >= 0) & (dist < WINDOW), -slope * dist.astype(F32), NEG_INF)


def _swa_body(slopes_ref, sinks_ref, q_ref, k_ref, v_ref, o_ref, qaug_scr, bias_scr):
    i = pl.program_id(0)

    @pl.when(i == 0)
    def _():
        qaug_scr[...] = jnp.zeros_like(qaug_scr)
        for hd in range(SWA_Q_HEADS):
            bias_scr[hd] = _swa_bias(slopes_ref[hd], WINDOW)

    def block(t, slab0, bias_of):
        kwin = k_ref[pl.ds(pl.multiple_of(slab0 * WINDOW, WINDOW), 2 * WINDOW), :]
        vt = jnp.concatenate([v_ref[slab0], v_ref[slab0 + 1]], axis=1)
        ones = jnp.ones((16, 2 * WINDOW), BF16)
        qs = q_ref[t]
        zs = []
        for kh in range(SWA_KV_HEADS):
            r0 = kh * SWA_HEAD_DIM
            for g in range(SWA_GROUP):
                hd = kh * SWA_GROUP + g
                qh = qs[hd * SWA_HEAD_DIM:(hd + 1) * SWA_HEAD_DIM, :].astype(F32) * (SWA_HEAD_DIM ** -0.5 * LOG2E)
                qaug_scr[kh, r0:r0 + SWA_HEAD_DIM, g * LANES:(g + 1) * LANES] = qh.astype(BF16)
            zs.append(_dot(kwin, qaug_scr[kh]))
        outs = []
        for kh in range(SWA_KV_HEADS):
            r0 = kh * SWA_HEAD_DIM
            z = zs[kh]
            vt_kh = jnp.concatenate([vt[r0:r0 + SWA_HEAD_DIM, :], ones], axis=0)
            row = []
            for g in range(SWA_GROUP):
                hd = kh * SWA_GROUP + g
                sink = sinks_ref[hd]
                sc = z[:, g * LANES:(g + 1) * LANES] + bias_of(hd)
                m = jnp.maximum(jnp.max(sc, axis=0, keepdims=True), sink)
                p = jnp.exp2(sc - m).astype(BF16)
                ot = _dot(vt_kh, p)
                denom = ot[SWA_HEAD_DIM:SWA_HEAD_DIM + 1, :] + jnp.exp2(sink - m)
                row.append(ot[0:SWA_HEAD_DIM, :] * (1.0 / denom))
            outs.append(row)
        r = pl.multiple_of(t * WINDOW, WINDOW)
        for pr in range(SWA_KV_HEADS // 2):
            for g in range(SWA_GROUP):
                x = jnp.concatenate([outs[2 * pr][g], outs[2 * pr + 1][g]], axis=0)
                c0 = (pr * SWA_GROUP + g) * LANES
                o_ref[pl.ds(r, WINDOW), c0:c0 + LANES] = x.T.astype(BF16)

    @pl.when(i == 0)
    def _():
        block(0, 0, lambda hd: _swa_bias(slopes_ref[hd], 0))

    def body(t, carry):
        block(t, i * SWA_BLOCKS + t - 1, lambda hd: bias_scr[hd])
        return carry

    lax.fori_loop(jnp.where(i == 0, 1, 0), SWA_BLOCKS, body, 0)


def _swa_attn(slopes, sinks, proj, proj_t):
    q_row_blk = (2 * DIFF_WIDTH) // SWA_WIDTH
    k_col_blk = (N_GATE + DIFF_WIDTH) // SWA_KV_WIDTH
    v_row_blk = (2 * DIFF_WIDTH + SWA_WIDTH) // SWA_KV_WIDTH + 1
    return pl.pallas_call(
        _swa_body,
        grid=(SEQ // SWA_TQ,),
        in_specs=[
            pl.BlockSpec(memory_space=pltpu.SMEM),
            pl.BlockSpec(memory_space=pltpu.SMEM),
            pl.BlockSpec((SWA_BLOCKS, SWA_WIDTH, LANES), lambda i: (i, q_row_blk, 0)),
            pl.BlockSpec((SEQ, SWA_KV_WIDTH), lambda i: (0, k_col_blk)),
            pl.BlockSpec((N_SLABS, SWA_KV_WIDTH, LANES), lambda i: (0, v_row_blk, 0)),
        ],
        out_specs=pl.BlockSpec((SWA_TQ, SWA_WIDTH), lambda i: (i, 0)),
        out_shape=jax.ShapeDtypeStruct((SEQ, SWA_WIDTH), BF16),
        scratch_shapes=[pltpu.VMEM((SWA_KV_HEADS, SWA_KV_WIDTH, SWA_GROUP * LANES), BF16),
                        pltpu.VMEM((SWA_Q_HEADS, 2 * WINDOW, WINDOW), F32)],
        compiler_params=_params("arbitrary"),
        name="swa_attn",
    )(slopes, sinks, proj_t, proj, proj_t)


def _mix_body(a_ref, b_ref, ga_ref, gb_ref, wa_ref, wb_ref, wo_ref, x_ref, gpost_ref, o_ref):
    j = pl.program_id(1)
    last = pl.num_programs(1) - 1
    tm = o_ref.shape[0]

    def step(chunk, first, final):
        for r in range(0, tm, chunk):
            rows = slice(r, r + chunk)
            ya = _dot(a_ref[rows, :], wa_ref[...])
            yb = _dot(b_ref[rows, :], wb_ref[...])
            mixed = ga_ref[rows, :].astype(F32) * ya + gb_ref[rows, :].astype(F32) * yb
            y = _dot(mixed.astype(BF16), wo_ref[...])
            if not first:
                y = o_ref[rows, :] + y
            if final:
                y = x_ref[rows, :] + _rms(y) * gpost_ref[...]
            o_ref[rows, :] = y

    pl.when(j == 0)(lambda: step(tm, True, False))
    pl.when((j > 0) & (j < last))(lambda: step(tm, False, False))
    pl.when(j == last)(lambda: step(FFN_EDGE_ROWS, False, True))


def _mix_out(a, b, proj, wa, wb, wo, x, gpost, *, tm=512, tn=1024):
    s, d = x.shape
    nj = d // tn
    return pl.pallas_call(
        _mix_body,
        grid=(s // tm, nj),
        in_specs=[
            pl.BlockSpec((tm, DIFF_WIDTH), lambda i, j: (i, 0)),
            pl.BlockSpec((tm, SWA_WIDTH), lambda i, j: (i, 0)),
            pl.BlockSpec((tm, tn), lambda i, j: (i, j)),
            pl.BlockSpec((tm, tn), lambda i, j: (i, nj + j)),
            pl.BlockSpec((DIFF_WIDTH, tn), lambda i, j: (0, j)),
            pl.BlockSpec((SWA_WIDTH, tn), lambda i, j: (0, j)),
            pl.BlockSpec((tn, d), lambda i, j: (j, 0)),
            pl.BlockSpec((tm, d), lambda i, j: (i, 0)),
            pl.BlockSpec((1, d), lambda i, j: (0, 0)),
        ],
        out_specs=pl.BlockSpec((tm, d), lambda i, j: (i, 0)),
        out_shape=jax.ShapeDtypeStruct((s, d), F32),
        compiler_params=_params("arbitrary", "arbitrary"),
        name="mix_out",
    )(a, b, proj, proj, wa, wb, wo, x, gpost)


def _alibi_slopes(n):
    return jnp.asarray(2.0 ** (-8.0 * np.arange(1, n + 1) / n), dtype=F32)


def kernel(x, ffn1_norm_pre, ffn1_w_in, ffn1_w_out, ffn1_norm_post, mix_norm_pre, w_in, diff_lambda, diff_subln, swa_sinks, w_branch_diff, w_branch_swa, w_out, mix_norm_post, ffn2_norm_pre, ffn2_w_in, ffn2_w_out, ffn2_norm_post):
    depth = ffn1_w_in.shape[0]
    xs = x.reshape(SEQ, D_MODEL)
    for l in range(depth):
        lam_init = 0.8 - 0.6 * math.exp(-0.3 * l)
        w = w_in[l]
        wbs = (w_branch_swa[l].reshape(2, 2, SWA_GROUP, SWA_HEAD_DIM, D_MODEL)
               .transpose(0, 2, 1, 3, 4).reshape(SWA_WIDTH, D_MODEL).astype(BF16))
        row = lambda v: v.reshape(1, -1)

        x1, h2 = _ffn(xs, row(ffn1_norm_pre[l]), ffn1_w_in[l].astype(BF16), ffn1_w_out[l].astype(BF16),
                      row(ffn1_norm_post[l]), row(mix_norm_pre[l]))
        proj, w2_in, w2_out = _proj(h2, w, convert=(ffn2_w_in[l], ffn2_w_out[l]))
        proj_t, wbd, wo = _proj_t(h2, w, convert=(w_branch_diff[l], w_out[l]))
        a = _diff_attn(_alibi_slopes(DIFF_HEADS) * LOG2E, diff_lambda[l], row(diff_subln[l]), proj, proj_t,
                       lam_init=lam_init)
        b = _swa_attn(_alibi_slopes(SWA_Q_HEADS) * LOG2E, swa_sinks[l] * LOG2E, proj, proj_t)
        x2 = _mix_out(a, b, proj, wbd, wbs, wo, x1, row(mix_norm_post[l]))
        (xs,) = _ffn(x2, row(ffn2_norm_pre[l]), w2_in, w2_out, row(ffn2_norm_post[l]))
    return xs.reshape(x.shape)
```

```python
import functools
import math

import numpy as np
import jax
import jax.numpy as jnp
from jax import lax
from jax.experimental import pallas as pl
from jax.experimental.pallas import tpu as pltpu

D_MODEL = 2048
SEQ = 8192
DIFF_HEADS = 8
DIFF_HEAD_DIM = 64
DIFF_V_DIM = 128
DIFF_WIDTH = 1024
SWA_Q_HEADS = 16
SWA_KV_HEADS = 4
SWA_GROUP = 4
SWA_HEAD_DIM = 64
SWA_WIDTH = 1024
SWA_KV_WIDTH = 256
WINDOW = 128
D_FF = 5504
NORM_EPS = 1e-6
NEG_INF = -1e30

LANES = 128
MXU_DIM = 256
VMEM_LIMIT_BYTES = 60 * 1024 * 1024

N_GATE = 2 * D_MODEL
PROJ_COLS = N_GATE + DIFF_WIDTH + 2 * SWA_KV_WIDTH
PROJT_ROWS = DIFF_WIDTH + DIFF_WIDTH + SWA_WIDTH + 2 * SWA_KV_WIDTH
N_SLABS = SEQ // LANES

BF16 = jnp.bfloat16
F32 = jnp.float32


def _dot(a, b):
    return jnp.dot(a, b, preferred_element_type=F32)


def _rms(x):
    return x * lax.rsqrt(jnp.mean(x * x, axis=-1, keepdims=True) + NORM_EPS)


def _params(*sem, flags=None):
    return pltpu.CompilerParams(dimension_semantics=sem, vmem_limit_bytes=VMEM_LIMIT_BYTES, flags=flags)


def _rider_specs(mats, n_steps, step_of):
    specs = []
    for m in mats:
        rb = 16 * pl.cdiv(m.shape[0], 16 * n_steps)
        while m.shape[0] % rb:
            rb += 16
        specs.append(pl.BlockSpec((rb, m.shape[1]),
                                  lambda *ids, nb=m.shape[0] // rb: (jnp.minimum(step_of(*ids), nb - 1), 0)))
    return specs


def _round_riders(srcs, dsts):
    for src, dst in zip(srcs, dsts):
        dst[...] = src[...].astype(BF16)


FFN_ROW_CHUNK = 512
FFN_EDGE_ROWS = 256


def _ffn_body(*refs, emit_next, overlap, n_convert):
    refs = list(refs)
    x_ref, gpre_ref, wg_ref, wu_ref, wo_ref, gpost_ref = refs[:6]
    del refs[:6]
    gnext_ref = refs.pop(0) if emit_next else None
    cvt_in = [refs.pop(0) for _ in range(n_convert)]
    o_ref = refs.pop(0)
    hn_ref = refs.pop(0) if emit_next else None
    cvt_out = [refs.pop(0) for _ in range(n_convert)]
    (h_scr,) = refs
    j = pl.program_id(1)
    last = pl.num_programs(1) - 1
    tm = h_scr.shape[0]

    def step(chunk, first, final):
        _round_riders(cvt_in, cvt_out)
        for r in range(0, tm, chunk):
            rows = slice(r, r + chunk)
            if first:
                h = (_rms(x_ref[rows, :]) * gpre_ref[...]).astype(BF16)
                h_scr[rows, :] = h
            else:
                h = h_scr[rows, :]
            g = _dot(h, wg_ref[...])
            u = _dot(h, wu_ref[...])
            hg = 0.5 * g
            act = (hg + hg * jnp.tanh(hg)) * u
            if final and overlap:
                col = lax.broadcasted_iota(jnp.int32, act.shape, 1)
                act = jnp.where(col < overlap, 0.0, act)
            y = _dot(act.astype(BF16), wo_ref[...])
            if not first:
                y = o_ref[rows, :] + y
            if final:
                y = x_ref[rows, :] + 0.5 * (_rms(y) * gpost_ref[...])
                if emit_next:
                    hn_ref[rows, :] = (_rms(y) * gnext_ref[...]).astype(BF16)
            o_ref[rows, :] = y

    pl.when(j == 0)(lambda: step(FFN_EDGE_ROWS, True, False))
    pl.when((j > 0) & (j < last))(lambda: step(FFN_ROW_CHUNK, False, False))
    pl.when(j == last)(lambda: step(FFN_EDGE_ROWS, False, True))


def _ffn(x, gpre, w_in, w_out, gpost, gnext=None, convert=(), *, tm=1024, tf=512):
    emit_next = gnext is not None
    s, d = x.shape
    d_ff = w_out.shape[0]
    nj = pl.cdiv(d_ff, tf)
    overlap = nj * tf - d_ff
    assert tf % LANES == 0 and d_ff % LANES == 0
    start = lambda j, base=0: (jnp.minimum(j * (tf // LANES), (d_ff - tf) // LANES) + base // LANES) * LANES
    row = lambda i, j: (i, 0)
    fixed = lambda i, j: (0, 0)
    in_specs = [
        pl.BlockSpec((tm, d), row),
        pl.BlockSpec((1, d), fixed),
        pl.BlockSpec((pl.Element(d), pl.Element(tf)), lambda i, j: (0, start(j))),
        pl.BlockSpec((pl.Element(d), pl.Element(tf)), lambda i, j: (0, start(j, d_ff))),
        pl.BlockSpec((pl.Element(tf), pl.Element(d)), lambda i, j: (start(j), 0)),
        pl.BlockSpec((1, d), fixed),
    ]
    args = [x, gpre, w_in, w_in, w_out, gpost]
    out_shape = [jax.ShapeDtypeStruct((s, d), F32)]
    out_specs = [pl.BlockSpec((tm, d), row)]
    if emit_next:
        in_specs.append(pl.BlockSpec((1, d), fixed))
        args.append(gnext)
        out_shape.append(jax.ShapeDtypeStruct((s, d), BF16))
        out_specs.append(pl.BlockSpec((tm, d), row, pipeline_mode=pl.Buffered(1)))
    riders = _rider_specs(convert, (s // tm) * nj, lambda i, j: i * nj + j)
    in_specs += riders
    args += list(convert)
    out_specs += riders
    out_shape += [jax.ShapeDtypeStruct(m.shape, BF16) for m in convert]
    res = pl.pallas_call(
        functools.partial(_ffn_body, emit_next=emit_next, overlap=overlap, n_convert=len(convert)),
        grid=(s // tm, nj),
        in_specs=in_specs,
        out_specs=out_specs,
        out_shape=out_shape,
        scratch_shapes=[pltpu.VMEM((tm, d), BF16)],
        compiler_params=_params("arbitrary", "arbitrary"),
        name="ffn_next" if emit_next else "ffn",
    )(*args)
    return res


PROJ_TN = 2 * MXU_DIM
assert 2 * SWA_KV_WIDTH == PROJ_TN
_O_DQ, _O_DK, _O_DV, _O_SQ = (k * DIFF_WIDTH // PROJ_TN for k in range(4))
_O_SKV = _O_SQ + SWA_WIDTH // PROJ_TN
_O_G = _O_SKV + 1
_N_G, _N_D, _N_S = N_GATE // PROJ_TN, DIFF_WIDTH // PROJ_TN, SWA_WIDTH // PROJ_TN


def _proj_src_tile(t):
    return jnp.where(t < _N_G, _O_G + t, jnp.where(t < _N_G + _N_D, _O_DK + (t - _N_G), _O_SKV))


def _proj_t_src_tile(t):
    return jnp.where(t < _N_D, _O_DQ + t,
                     jnp.where(t < 2 * _N_D, _O_DV + (t - _N_D),
                               jnp.where(t < 2 * _N_D + _N_S, _O_SQ + (t - 2 * _N_D), _O_SKV)))


PROJ_ROW_CHUNK = 256


def _proj_body(h_ref, w_ref, *refs):
    n = (len(refs) - 2) // 2
    cvt_in, o_ref, cvt_out, w_scr = refs[:n], refs[n], refs[n + 1:2 * n + 1], refs[2 * n + 1]

    def run(gated):
        w_scr[...] = w_ref[...].astype(BF16)
        _round_riders(cvt_in, cvt_out)
        for r in range(0, h_ref.shape[0], PROJ_ROW_CHUNK):
            rows = slice(r, r + PROJ_ROW_CHUNK)
            res = _dot(h_ref[rows, :], w_scr[...])
            if gated:
                res = 0.5 * jnp.tanh(0.5 * res) + 0.5
            o_ref[rows, :] = res.astype(BF16)

    gated = pl.program_id(1) < _N_G
    pl.when(gated)(lambda: run(True))
    pl.when(jnp.logical_not(gated))(lambda: run(False))


def _proj(h, w, convert=(), *, tm=2048):
    s, d = h.shape
    tn = PROJ_TN
    nt = PROJ_COLS // tn
    riders = _rider_specs(convert, (s // tm) * nt, lambda i, t: i * nt + t)
    return pl.pallas_call(
        _proj_body,
        grid=(s // tm, nt),
        in_specs=[pl.BlockSpec((tm, d), lambda i, t: (i, 0)),
                  pl.BlockSpec((d, tn), lambda i, t: (0, _proj_src_tile(t)))] + riders,
        out_specs=[pl.BlockSpec((tm, tn), lambda i, t: (i, t))] + riders,
        out_shape=[jax.ShapeDtypeStruct((s, PROJ_COLS), BF16)]
                  + [jax.ShapeDtypeStruct(m.shape, BF16) for m in convert],
        scratch_shapes=[pltpu.VMEM((d, tn), BF16)],
        compiler_params=_params("arbitrary", "arbitrary"),
        name="proj",
    )(h, w, *convert)


def _proj_t_body(h_ref, w_ref, *refs):
    n = (len(refs) - 2) // 2
    cvt_in, o_ref, cvt_out, w_scr = refs[:n], refs[n], refs[n + 1:2 * n + 1], refs[2 * n + 1]
    w_scr[...] = w_ref[...].astype(BF16)
    _round_riders(cvt_in, cvt_out)
    for r in range(0, h_ref.shape[0], PROJ_ROW_CHUNK):
        res = _dot(h_ref[r:r + PROJ_ROW_CHUNK, :], w_scr[...])
        for k in range(PROJ_ROW_CHUNK // LANES):
            o_ref[r // LANES + k] = res[k * LANES:(k + 1) * LANES, :].T.astype(BF16)


def _proj_t(h, w, convert=(), *, tm=2048):
    s, d = h.shape
    tn = PROJ_TN
    nt = PROJT_ROWS // tn
    riders = _rider_specs(convert, (s // tm) * nt, lambda i, t: i * nt + t)
    return pl.pallas_call(
        _proj_t_body,
        grid=(s // tm, nt),
        in_specs=[pl.BlockSpec((tm, d), lambda i, t: (i, 0)),
                  pl.BlockSpec((d, tn), lambda i, t: (0, _proj_t_src_tile(t)))] + riders,
        out_specs=[pl.BlockSpec((tm // LANES, tn, LANES), lambda i, t: (i, t, 0))] + riders,
        out_shape=[jax.ShapeDtypeStruct((s // LANES, PROJT_ROWS, LANES), BF16)]
                  + [jax.ShapeDtypeStruct(m.shape, BF16) for m in convert],
        scratch_shapes=[pltpu.VMEM((d, tn), BF16)],
        compiler_params=_params("arbitrary", "arbitrary"),
        name="proj_t",
    )(h, w, *convert)


DIFF_TQ = 512
DIFF_TK = 256
DIFF_ACC_ROWS = DIFF_V_DIM + 16
DIFF_HPS = 4
LOG2E = math.log2(math.e)


def _diff_body(slopes_ref, lam_ref, q_ref, k_ref, v_ref, e_ref, g_ref, o_ref, qaug_scr, *scr, lam_init):
    tq, tk, nh = DIFF_TQ, DIFF_TK, DIFF_TQ // DIFF_TK
    hp = pl.program_id(0)
    i = pl.program_id(1)
    z_scrs, zmax_scrs = scr[:DIFF_HPS], scr[DIFF_HPS:2 * DIFF_HPS]
    m_scr, acc_scr = scr[2 * DIFF_HPS:]
    slopes = [slopes_ref[hp * DIFF_HPS + hh] for hh in range(DIFF_HPS)]

    row = lax.broadcasted_iota(jnp.int32, (LANES, tq), 0)

    def build_queries(hh):
        q = jnp.concatenate([q_ref[s, hh * LANES:(hh + 1) * LANES, :] for s in range(tq // LANES)], axis=1)
        qs = (q.astype(F32) * (DIFF_HEAD_DIM ** -0.5 * LOG2E)).astype(BF16)
        zero = jnp.zeros_like(qs)
        s_f32 = jnp.where(row < 2, slopes[hh], jnp.where(row < 4, slopes[hh] * tk, 0.0))
        s_hi = s_f32.astype(BF16)
        s_lo = (s_f32 - s_hi.astype(F32)).astype(BF16)
        srow = jnp.where(row % 2 == 0, s_hi, s_lo)
        qaug_scr[2 * hh, 0:LANES, :] = jnp.where(row < DIFF_HEAD_DIM, qs, zero)
        qaug_scr[2 * hh + 1, 0:LANES, :] = jnp.where(row >= DIFF_HEAD_DIM, qs, zero)
        qaug_scr[2 * hh, LANES:2 * LANES, :] = srow
        qaug_scr[2 * hh + 1, LANES:2 * LANES, :] = srow

    def qk(hh, j):
        for s in range(nh):
            rows = pl.ds(pl.multiple_of(j * tq + s * tk, tk), tk)
            kc = jnp.concatenate([k_ref[rows, hh * LANES:(hh + 1) * LANES],
                                  e_ref[s * tk:(s + 1) * tk, :]], axis=1)
            for c in range(2):
                z = _dot(kc, qaug_scr[2 * hh + c])
                z_scrs[hh][c * nh + s] = z
                zmax_scrs[hh][c * nh + s] = jnp.max(z, axis=0, keepdims=True)

    def softmax_pv(hh, j, diagonal):
        vt = jnp.concatenate([v_ref[(tq // LANES) * j + s, hh * LANES:(hh + 1) * LANES, :]
                              for s in range(tq // LANES)], axis=1)
        vt = jnp.concatenate([vt, jnp.ones((DIFF_ACC_ROWS - DIFF_V_DIM, tq), BF16)], axis=0)
        koff = (jnp.zeros((1, tq), jnp.int32) + j * tq).astype(F32) * slopes[hh]
        kr = lax.broadcasted_iota(jnp.int32, (tk, tq), 0)
        qc = lax.broadcasted_iota(jnp.int32, (tk, tq), 1)
        for c in range(2):
            zs = [z_scrs[hh][c * nh + s] for s in range(nh)]
            if diagonal:
                zs = [jnp.where(kr + s * tk <= qc, z, NEG_INF) for s, z in enumerate(zs)]
                zmaxs = [jnp.max(z, axis=0, keepdims=True) for z in zs]
            else:
                zmaxs = [zmax_scrs[hh][c * nh + s] for s in range(nh)]
            m_old = m_scr[2 * hh + c]
            m_tile = functools.reduce(jnp.maximum, zmaxs)
            m_new = jnp.maximum(m_old, m_tile + koff)
            alpha = jnp.exp2(m_old - m_new)
            shift = m_new - koff
            p = jnp.concatenate([jnp.exp2(z - shift).astype(BF16) for z in zs], axis=0)
            acc_scr[2 * hh + c] = alpha * acc_scr[2 * hh + c] + _dot(vt, p)
            m_scr[2 * hh + c] = m_new

    group_a = range(0, DIFF_HPS // 2)
    group_b = range(DIFF_HPS // 2, DIFF_HPS)
    for hh in group_a:
        build_queries(hh)
    for hh in group_a:
        qk(hh, 0)
    for hh in group_b:
        build_queries(hh)
    m_scr[...] = jnp.full_like(m_scr, NEG_INF)
    acc_scr[...] = jnp.zeros_like(acc_scr)

    def body(j, carry):
        for hh in group_b:
            qk(hh, j)
        for hh in group_a:
            softmax_pv(hh, j, False)
        for hh in group_a:
            qk(hh, j + 1)
        for hh in group_b:
            softmax_pv(hh, j, False)
        return carry

    lax.fori_loop(0, i, body, 0)
    for hh in group_b:
        qk(hh, i)
    for hh in range(DIFF_HPS):
        softmax_pv(hh, i, True)

    lp = lam_ref[...]
    lam = (jnp.exp(jnp.sum(lp[0:1, :] * lp[1:2, :], axis=1, keepdims=True))
           - jnp.exp(jnp.sum(lp[2:3, :] * lp[3:4, :], axis=1, keepdims=True)) + lam_init)
    for hh in range(DIFF_HPS):
        a0, a1 = acc_scr[2 * hh], acc_scr[2 * hh + 1]
        o0 = a0[0:DIFF_V_DIM, :] * (1.0 / a0[DIFF_V_DIM:DIFF_V_DIM + 1, :])
        o1 = a1[0:DIFF_V_DIM, :] * (1.0 / a1[DIFF_V_DIM:DIFF_V_DIM + 1, :])
        o = o0 - lam * o1
        y = o * lax.rsqrt(jnp.mean(o * o, axis=0, keepdims=True) + NORM_EPS)
        o_ref[:, hh * DIFF_V_DIM:(hh + 1) * DIFF_V_DIM] = ((y.T * g_ref[...]) * (1.0 - lam_init)).astype(BF16)


def _diff_attn(slopes, lam_params, subln, proj, proj_t, *, lam_init):
    tq, tk, hps = DIFF_TQ, DIFF_TK, DIFF_HPS
    w = hps * LANES
    k_col0 = N_GATE // w
    v_row0 = DIFF_WIDTH // w
    r = np.arange(tq)[:, None]
    c = np.arange(LANES)[None, :]
    e = jnp.asarray(np.where(c < 2, r % tk, np.where(c < 4, r // tk, 0)), dtype=BF16)
    return pl.pallas_call(
        functools.partial(_diff_body, lam_init=lam_init),
        grid=(DIFF_HEADS // hps, SEQ // tq),
        in_specs=[
            pl.BlockSpec(memory_space=pltpu.SMEM),
            pl.BlockSpec((4, DIFF_HEAD_DIM), lambda h, i: (0, 0)),
            pl.BlockSpec((tq // LANES, w, LANES), lambda h, i: (i, h, 0)),
            pl.BlockSpec((SEQ, w), lambda h, i: (0, k_col0 + h)),
            pl.BlockSpec((N_SLABS, w, LANES), lambda h, i: (0, v_row0 + h, 0)),
            pl.BlockSpec((tq, LANES), lambda h, i: (0, 0)),
            pl.BlockSpec((1, DIFF_V_DIM), lambda h, i: (0, 0)),
        ],
        out_specs=pl.BlockSpec((tq, hps * DIFF_V_DIM), lambda h, i: (i, h)),
        out_shape=jax.ShapeDtypeStruct((SEQ, DIFF_WIDTH), BF16),
        scratch_shapes=[
            pltpu.VMEM((2 * hps, 2 * LANES, tq), BF16),
            *[pltpu.VMEM((2 * (tq // tk), tk, tq), F32)] * hps,
            *[pltpu.VMEM((2 * (tq // tk), 1, tq), F32)] * hps,
            pltpu.VMEM((2 * hps, 1, tq), F32),
            pltpu.VMEM((2 * hps, DIFF_ACC_ROWS, tq), F32),
        ],
        compiler_params=_params("arbitrary", "arbitrary"),
        name="diff_attn",
    )(slopes, lam_params, proj_t, proj, proj_t, e, subln)


SWA_TQ = 512
SWA_BLOCKS = SWA_TQ // WINDOW


def _swa_bias(slope, off):
    kb = lax.broadcasted_iota(jnp.int32, (2 * WINDOW, WINDOW), 0)
    qa = lax.broadcasted_iota(jnp.int32, (2 * WINDOW, WINDOW), 1)
    dist = off + qa - kb
    return jnp.where((dist >= 0) & (dist < WINDOW), -slope * dist.astype(F32), NEG_INF)


def _swa_body(slopes_ref, sinks_ref, q_ref, k_ref, v_ref, o_ref, qaug_scr, bias_scr):
    i = pl.program_id(0)

    @pl.when(i == 0)
    def _():
        qaug_scr[...] = jnp.zeros_like(qaug_scr)
        for hd in range(SWA_Q_HEADS):
            bias_scr[hd] = _swa_bias(slopes_ref[hd], WINDOW)

    def block(t, slab0, bias_of):
        kwin = k_ref[pl.ds(pl.multiple_of(slab0 * WINDOW, WINDOW), 2 * WINDOW), :]
        vt = jnp.concatenate([v_ref[slab0], v_ref[slab0 + 1]], axis=1)
        ones = jnp.ones((16, 2 * WINDOW), BF16)
        qs = q_ref[t]
        zs = []
        for kh in range(SWA_KV_HEADS):
            r0 = kh * SWA_HEAD_DIM
            for g in range(SWA_GROUP):
                hd = kh * SWA_GROUP + g
                qh = qs[hd * SWA_HEAD_DIM:(hd + 1) * SWA_HEAD_DIM, :].astype(F32) * (SWA_HEAD_DIM ** -0.5 * LOG2E)
                qaug_scr[kh, r0:r0 + SWA_HEAD_DIM, g * LANES:(g + 1) * LANES] = qh.astype(BF16)
            zs.append(_dot(kwin, qaug_scr[kh]))
        outs = []
        for kh in range(SWA_KV_HEADS):
            r0 = kh * SWA_HEAD_DIM
            z = zs[kh]
            vt_kh = jnp.concatenate([vt[r0:r0 + SWA_HEAD_DIM, :], ones], axis=0)
            row = []
            for g in range(SWA_GROUP):
                hd = kh * SWA_GROUP + g
                sink = sinks_ref[hd]
                sc = z[:, g * LANES:(g + 1) * LANES] + bias_of(hd)
                m = jnp.maximum(jnp.max(sc, axis=0, keepdims=True), sink)
                p = jnp.exp2(sc - m).astype(BF16)
                ot = _dot(vt_kh, p)
                denom = ot[SWA_HEAD_DIM:SWA_HEAD_DIM + 1, :] + jnp.exp2(sink - m)
                row.append(ot[0:SWA_HEAD_DIM, :] * (1.0 / denom))
            outs.append(row)
        r = pl.multiple_of(t * WINDOW, WINDOW)
        for pr in range(SWA_KV_HEADS // 2):
            for g in range(SWA_GROUP):
                x = jnp.concatenate([outs[2 * pr][g], outs[2 * pr + 1][g]], axis=0)
                c0 = (pr * SWA_GROUP + g) * LANES
                o_ref[pl.ds(r, WINDOW), c0:c0 + LANES] = x.T.astype(BF16)

    @pl.when(i == 0)
    def _():
        block(0, 0, lambda hd: _swa_bias(slopes_ref[hd], 0))

    def body(t, carry):
        block(t, i * SWA_BLOCKS + t - 1, lambda hd: bias_scr[hd])
        return carry

    lax.fori_loop(jnp.where(i == 0, 1, 0), SWA_BLOCKS, body, 0)


def _swa_attn(slopes, sinks, proj, proj_t):
    q_row_blk = (2 * DIFF_WIDTH) // SWA_WIDTH
    k_col_blk = (N_GATE + DIFF_WIDTH) // SWA_KV_WIDTH
    v_row_blk = (2 * DIFF_WIDTH + SWA_WIDTH) // SWA_KV_WIDTH + 1
    return pl.pallas_call(
        _swa_body,
        grid=(SEQ // SWA_TQ,),
        in_specs=[
            pl.BlockSpec(memory_space=pltpu.SMEM),
            pl.BlockSpec(memory_space=pltpu.SMEM),
            pl.BlockSpec((SWA_BLOCKS, SWA_WIDTH, LANES), lambda i: (i, q_row_blk, 0)),
            pl.BlockSpec((SEQ, SWA_KV_WIDTH), lambda i: (0, k_col_blk)),
            pl.BlockSpec((N_SLABS, SWA_KV_WIDTH, LANES), lambda i: (0, v_row_blk, 0)),
        ],
        out_specs=pl.BlockSpec((SWA_TQ, SWA_WIDTH), lambda i: (i, 0)),
        out_shape=jax.ShapeDtypeStruct((SEQ, SWA_WIDTH), BF16),
        scratch_shapes=[pltpu.VMEM((SWA_KV_HEADS, SWA_KV_WIDTH, SWA_GROUP * LANES), BF16),
                        pltpu.VMEM((SWA_Q_HEADS, 2 * WINDOW, WINDOW), F32)],
        compiler_params=_params("arbitrary"),
        name="swa_attn",
    )(slopes, sinks, proj_t, proj, proj_t)


def _mix_body(a_ref, b_ref, ga_ref, gb_ref, wa_ref, wb_ref, wo_ref, x_ref, gpost_ref, o_ref):
    j = pl.program_id(1)
    last = pl.num_programs(1) - 1
    tm = o_ref.shape[0]

    def step(chunk, first, final):
        for r in range(0, tm, chunk):
            rows = slice(r, r + chunk)
            ya = _dot(a_ref[rows, :], wa_ref[...])
            yb = _dot(b_ref[rows, :], wb_ref[...])
            mixed = ga_ref[rows, :].astype(F32) * ya + gb_ref[rows, :].astype(F32) * yb
            y = _dot(mixed.astype(BF16), wo_ref[...])
            if not first:
                y = o_ref[rows, :] + y
            if final:
                y = x_ref[rows, :] + _rms(y) * gpost_ref[...]
            o_ref[rows, :] = y

    pl.when(j == 0)(lambda: step(tm, True, False))
    pl.when((j > 0) & (j < last))(lambda: step(tm, False, False))
    pl.when(j == last)(lambda: step(FFN_EDGE_ROWS, False, True))


def _mix_out(a, b, proj, wa, wb, wo, x, gpost, *, tm=512, tn=1024):
    s, d = x.shape
    nj = d // tn
    return pl.pallas_call(
        _mix_body,
        grid=(s // tm, nj),
        in_specs=[
            pl.BlockSpec((tm, DIFF_WIDTH), lambda i, j: (i, 0)),
            pl.BlockSpec((tm, SWA_WIDTH), lambda i, j: (i, 0)),
            pl.BlockSpec((tm, tn), lambda i, j: (i, j)),
            pl.BlockSpec((tm, tn), lambda i, j: (i, nj + j)),
            pl.BlockSpec((DIFF_WIDTH, tn), lambda i, j: (0, j)),
            pl.BlockSpec((SWA_WIDTH, tn), lambda i, j: (0, j)),
            pl.BlockSpec((tn, d), lambda i, j: (j, 0)),
            pl.BlockSpec((tm, d), lambda i, j: (i, 0)),
            pl.BlockSpec((1, d), lambda i, j: (0, 0)),
        ],
        out_specs=pl.BlockSpec((tm, d), lambda i, j: (i, 0)),
        out_shape=jax.ShapeDtypeStruct((s, d), F32),
        compiler_params=_params("arbitrary", "arbitrary"),
        name="mix_out",
    )(a, b, proj, proj, wa, wb, wo, x, gpost)


def _alibi_slopes(n):
    return jnp.asarray(2.0 ** (-8.0 * np.arange(1, n + 1) / n), dtype=F32)


def kernel(x, ffn1_norm_pre, ffn1_w_in, ffn1_w_out, ffn1_norm_post, mix_norm_pre, w_in, diff_lambda, diff_subln, swa_sinks, w_branch_diff, w_branch_swa, w_out, mix_norm_post, ffn2_norm_pre, ffn2_w_in, ffn2_w_out, ffn2_norm_post):
    depth = ffn1_w_in.shape[0]
    xs = x.reshape(SEQ, D_MODEL)
    for l in range(depth):
        lam_init = 0.8 - 0.6 * math.exp(-0.3 * l)
        w = w_in[l]
        wbs = (w_branch_swa[l].reshape(2, 2, SWA_GROUP, SWA_HEAD_DIM, D_MODEL)
               .transpose(0, 2, 1, 3, 4).reshape(SWA_WIDTH, D_MODEL).astype(BF16))
        row = lambda v: v.reshape(1, -1)

        x1, h2 = _ffn(xs, row(ffn1_norm_pre[l]), ffn1_w_in[l].astype(BF16), ffn1_w_out[l].astype(BF16),
                      row(ffn1_norm_post[l]), row(mix_norm_pre[l]))
        proj, w2_in, w2_out = _proj(h2, w, convert=(ffn2_w_in[l], ffn2_w_out[l]))
        proj_t, wbd, wo = _proj_t(h2, w, convert=(w_branch_diff[l], w_out[l]))
        a = _diff_attn(_alibi_slopes(DIFF_HEADS) * LOG2E, diff_lambda[l], row(diff_subln[l]), proj, proj_t,
                       lam_init=lam_init)
        b = _swa_attn(_alibi_slopes(SWA_Q_HEADS) * LOG2E, swa_sinks[l] * LOG2E, proj, proj_t)
        x2 = _mix_out(a, b, proj, wbd, wbs, wo, x1, row(mix_norm_post[l]))
        (xs,) = _ffn(x2, row(ffn2_norm_pre[l]), w2_in, w2_out, row(ffn2_norm_post[l]))
    return xs.reshape(x.shape)
```

```python
import functools
import math

import numpy as np
import jax
import jax.numpy as jnp
from jax import lax
from jax.experimental import pallas as pl
from jax.experimental.pallas import tpu as pltpu

D_MODEL = 2048
SEQ = 8192
DIFF_HEADS = 8
DIFF_HEAD_DIM = 64
DIFF_V_DIM = 128
DIFF_WIDTH = 1024
SWA_Q_HEADS = 16
SWA_KV_HEADS = 4
SWA_GROUP = 4
SWA_HEAD_DIM = 64
SWA_WIDTH = 1024
SWA_KV_WIDTH = 256
WINDOW = 128
D_FF = 5504
NORM_EPS = 1e-6
NEG_INF = -1e30

LANES = 128
MXU_DIM = 256
VMEM_LIMIT_BYTES = 60 * 1024 * 1024

N_GATE = 2 * D_MODEL
PROJ_COLS = N_GATE + DIFF_WIDTH + SWA_KV_WIDTH
PROJT_ROWS = DIFF_WIDTH + DIFF_WIDTH + SWA_WIDTH + SWA_KV_WIDTH
N_SLABS = SEQ // LANES

BF16 = jnp.bfloat16
F32 = jnp.float32


def _dot(a, b):
    return jnp.dot(a, b, preferred_element_type=F32)


def _rms(x):
    return x * lax.rsqrt(jnp.mean(x * x, axis=-1, keepdims=True) + NORM_EPS)


def _params(*sem, flags=None):
    return pltpu.CompilerParams(dimension_semantics=sem, vmem_limit_bytes=VMEM_LIMIT_BYTES, flags=flags)


def _rider_specs(mats, n_steps, step_of):
    specs = []
    for m in mats:
        rb = 16 * pl.cdiv(m.shape[0], 16 * n_steps)
        while m.shape[0] % rb:
            rb += 16
        specs.append(pl.BlockSpec((rb, m.shape[1]),
                                  lambda *ids, nb=m.shape[0] // rb: (jnp.minimum(step_of(*ids), nb - 1), 0)))
    return specs


def _round_riders(srcs, dsts):
    for src, dst in zip(srcs, dsts):
        dst[...] = src[...].astype(BF16)


FFN_ROW_CHUNK = 512
FFN_EDGE_ROWS = 256


def _ffn_body(*refs, emit_next, overlap, n_convert):
    refs = list(refs)
    x_ref, gpre_ref, wg_ref, wu_ref, wo_ref, gpost_ref = refs[:6]
    del refs[:6]
    gnext_ref = refs.pop(0) if emit_next else None
    cvt_in = [refs.pop(0) for _ in range(n_convert)]
    o_ref = refs.pop(0)
    hn_ref = refs.pop(0) if emit_next else None
    cvt_out = [refs.pop(0) for _ in range(n_convert)]
    (h_scr,) = refs
    j = pl.program_id(1)
    last = pl.num_programs(1) - 1
    tm = h_scr.shape[0]

    def step(chunk, first, final):
        _round_riders(cvt_in, cvt_out)
        for r in range(0, tm, chunk):
            rows = slice(r, r + chunk)
            if first:
                h = (_rms(x_ref[rows, :]) * gpre_ref[...]).astype(BF16)
                h_scr[rows, :] = h
            else:
                h = h_scr[rows, :]
            g = _dot(h, wg_ref[...])
            u = _dot(h, wu_ref[...])
            hg = 0.5 * g
            act = (hg + hg * jnp.tanh(hg)) * u
            if final and overlap:
                col = lax.broadcasted_iota(jnp.int32, act.shape, 1)
                act = jnp.where(col < overlap, 0.0, act)
            y = _dot(act.astype(BF16), wo_ref[...])
            if not first:
                y = o_ref[rows, :] + y
            if final:
                y = x_ref[rows, :] + 0.5 * (_rms(y) * gpost_ref[...])
                if emit_next:
                    hn_ref[rows, :] = (_rms(y) * gnext_ref[...]).astype(BF16)
            o_ref[rows, :] = y

    pl.when(j == 0)(lambda: step(FFN_EDGE_ROWS, True, False))
    pl.when((j > 0) & (j < last))(lambda: step(FFN_ROW_CHUNK, False, False))
    pl.when(j == last)(lambda: step(FFN_EDGE_ROWS, False, True))


def _ffn(x, gpre, w_in, w_out, gpost, gnext=None, convert=(), *, tm=1024, tf=512):
    emit_next = gnext is not None
    s, d = x.shape
    d_ff = w_out.shape[0]
    nj = pl.cdiv(d_ff, tf)
    overlap = nj * tf - d_ff
    assert tf % LANES == 0 and d_ff % LANES == 0
    start = lambda j, base=0: (jnp.minimum(j * (tf // LANES), (d_ff - tf) // LANES) + base // LANES) * LANES
    row = lambda i, j: (i, 0)
    fixed = lambda i, j: (0, 0)
    in_specs = [
        pl.BlockSpec((tm, d), row),
        pl.BlockSpec((1, d), fixed),
        pl.BlockSpec((pl.Element(d), pl.Element(tf)), lambda i, j: (0, start(j))),
        pl.BlockSpec((pl.Element(d), pl.Element(tf)), lambda i, j: (0, start(j, d_ff))),
        pl.BlockSpec((pl.Element(tf), pl.Element(d)), lambda i, j: (start(j), 0)),
        pl.BlockSpec((1, d), fixed),
    ]
    args = [x, gpre, w_in, w_in, w_out, gpost]
    out_shape = [jax.ShapeDtypeStruct((s, d), F32)]
    out_specs = [pl.BlockSpec((tm, d), row)]
    if emit_next:
        in_specs.append(pl.BlockSpec((1, d), fixed))
        args.append(gnext)
        out_shape.append(jax.ShapeDtypeStruct((s, d), BF16))
        out_specs.append(pl.BlockSpec((tm, d), row, pipeline_mode=pl.Buffered(1)))
    riders = _rider_specs(convert, (s // tm) * nj, lambda i, j: i * nj + j)
    in_specs += riders
    args += list(convert)
    out_specs += riders
    out_shape += [jax.ShapeDtypeStruct(m.shape, BF16) for m in convert]
    res = pl.pallas_call(
        functools.partial(_ffn_body, emit_next=emit_next, overlap=overlap, n_convert=len(convert)),
        grid=(s // tm, nj),
        in_specs=in_specs,
        out_specs=out_specs,
        out_shape=out_shape,
        scratch_shapes=[pltpu.VMEM((tm, d), BF16)],
        compiler_params=_params("arbitrary", "arbitrary"),
        name="ffn_next" if emit_next else "ffn",
    )(*args)
    return res


PROJ_TN = 2 * MXU_DIM
assert 2 * SWA_KV_WIDTH == PROJ_TN
_O_DQ, _O_DK, _O_DV, _O_SQ = (k * DIFF_WIDTH // PROJ_TN for k in range(4))
_O_SKV = _O_SQ + SWA_WIDTH // PROJ_TN
_O_G = _O_SKV + 1
_N_G, _N_D, _N_S = N_GATE // PROJ_TN, DIFF_WIDTH // PROJ_TN, SWA_WIDTH // PROJ_TN


def _proj_src_tile(t):
    return jnp.where(t < _N_G, _O_G + t, jnp.where(t < _N_G + _N_D, _O_DK + (t - _N_G), _O_SKV))


def _proj_t_src_tile(t):
    return jnp.where(t < _N_D, _O_DQ + t,
                     jnp.where(t < 2 * _N_D, _O_DV + (t - _N_D),
                               jnp.where(t < 2 * _N_D + _N_S, _O_SQ + (t - 2 * _N_D), _O_SKV)))


PROJ_ROW_CHUNK = 256


def _proj_body(h_ref, w_ref, *refs):
    n = (len(refs) - 2) // 2
    cvt_in, o_ref, cvt_out, w_scr = refs[:n], refs[n], refs[n + 1:2 * n + 1], refs[2 * n + 1]

    def run(gated, width):
        w_scr[...] = w_ref[...].astype(BF16)
        _round_riders(cvt_in, cvt_out)
        for r in range(0, h_ref.shape[0], PROJ_ROW_CHUNK):
            rows = slice(r, r + PROJ_ROW_CHUNK)
            res = _dot(h_ref[rows, :], w_scr[:, 0:width])
            if gated:
                res = 0.5 * jnp.tanh(0.5 * res) + 0.5
            o_ref[rows, 0:width] = res.astype(BF16)

    t = pl.program_id(1)
    last = pl.num_programs(1) - 1
    pl.when(t < _N_G)(lambda: run(True, PROJ_TN))
    pl.when((t >= _N_G) & (t < last))(lambda: run(False, PROJ_TN))
    pl.when(t == last)(lambda: run(False, SWA_KV_WIDTH))


def _proj(h, w, convert=(), *, tm=2048):
    s, d = h.shape
    tn = PROJ_TN
    nt = pl.cdiv(PROJ_COLS, tn)
    riders = _rider_specs(convert, (s // tm) * nt, lambda i, t: i * nt + t)
    return pl.pallas_call(
        _proj_body,
        grid=(s // tm, nt),
        in_specs=[pl.BlockSpec((tm, d), lambda i, t: (i, 0)),
                  pl.BlockSpec((d, tn), lambda i, t: (0, _proj_src_tile(t)))] + riders,
        out_specs=[pl.BlockSpec((tm, tn), lambda i, t: (i, t))] + riders,
        out_shape=[jax.ShapeDtypeStruct((s, PROJ_COLS), BF16)]
                  + [jax.ShapeDtypeStruct(m.shape, BF16) for m in convert],
        scratch_shapes=[pltpu.VMEM((d, tn), BF16)],
        compiler_params=_params("arbitrary", "arbitrary"),
        name="proj",
    )(h, w, *convert)


def _proj_t_body(h_ref, w_ref, *refs):
    n = (len(refs) - 2) // 2
    cvt_in, o_ref, cvt_out, w_scr = refs[:n], refs[n], refs[n + 1:2 * n + 1], refs[2 * n + 1]
    def run(c0, width):
        w_scr[...] = w_ref[...].astype(BF16)
        _round_riders(cvt_in, cvt_out)
        for r in range(0, h_ref.shape[0], PROJ_ROW_CHUNK):
            res = _dot(h_ref[r:r + PROJ_ROW_CHUNK, :], w_scr[:, c0:c0 + width])
            for k in range(PROJ_ROW_CHUNK // LANES):
                o_ref[r // LANES + k, 0:width, :] = res[k * LANES:(k + 1) * LANES, :].T.astype(BF16)

    t = pl.program_id(1)
    last = pl.num_programs(1) - 1
    pl.when(t < last)(lambda: run(0, PROJ_TN))
    pl.when(t == last)(lambda: run(SWA_KV_WIDTH, SWA_KV_WIDTH))


def _proj_t(h, w, convert=(), *, tm=2048):
    s, d = h.shape
    tn = PROJ_TN
    nt = pl.cdiv(PROJT_ROWS, tn)
    riders = _rider_specs(convert, (s // tm) * nt, lambda i, t: i * nt + t)
    return pl.pallas_call(
        _proj_t_body,
        grid=(s // tm, nt),
        in_specs=[pl.BlockSpec((tm, d), lambda i, t: (i, 0)),
                  pl.BlockSpec((d, tn), lambda i, t: (0, _proj_t_src_tile(t)))] + riders,
        out_specs=[pl.BlockSpec((tm // LANES, tn, LANES), lambda i, t: (i, t, 0))] + riders,
        out_shape=[jax.ShapeDtypeStruct((s // LANES, PROJT_ROWS, LANES), BF16)]
                  + [jax.ShapeDtypeStruct(m.shape, BF16) for m in convert],
        scratch_shapes=[pltpu.VMEM((d, tn), BF16)],
        compiler_params=_params("arbitrary", "arbitrary"),
        name="proj_t",
    )(h, w, *convert)


DIFF_TQ = 512
DIFF_TK = 256
DIFF_ACC_ROWS = DIFF_V_DIM + 16
DIFF_HPS = 4
LOG2E = math.log2(math.e)


def _diff_body(slopes_ref, lam_ref, q_ref, k_ref, v_ref, e_ref, g_ref, o_ref, qaug_scr, *scr, lam_init):
    tq, tk, nh = DIFF_TQ, DIFF_TK, DIFF_TQ // DIFF_TK
    hp = pl.program_id(0)
    i = pl.program_id(1)
    z_scrs, zmax_scrs = scr[:DIFF_HPS], scr[DIFF_HPS:2 * DIFF_HPS]
    m_scr, acc_scr = scr[2 * DIFF_HPS:]
    slopes = [slopes_ref[hp * DIFF_HPS + hh] for hh in range(DIFF_HPS)]

    row = lax.broadcasted_iota(jnp.int32, (LANES, tq), 0)

    def build_queries(hh):
        q = jnp.concatenate([q_ref[s, hh * LANES:(hh + 1) * LANES, :] for s in range(tq // LANES)], axis=1)
        qs = (q.astype(F32) * (DIFF_HEAD_DIM ** -0.5 * LOG2E)).astype(BF16)
        zero = jnp.zeros_like(qs)
        s_f32 = jnp.where(row < 2, slopes[hh], jnp.where(row < 4, slopes[hh] * tk, 0.0))
        s_hi = s_f32.astype(BF16)
        s_lo = (s_f32 - s_hi.astype(F32)).astype(BF16)
        srow = jnp.where(row % 2 == 0, s_hi, s_lo)
        qaug_scr[2 * hh, 0:LANES, :] = jnp.where(row < DIFF_HEAD_DIM, qs, zero)
        qaug_scr[2 * hh + 1, 0:LANES, :] = jnp.where(row >= DIFF_HEAD_DIM, qs, zero)
        qaug_scr[2 * hh, LANES:2 * LANES, :] = srow
        qaug_scr[2 * hh + 1, LANES:2 * LANES, :] = srow

    def qk(hh, j):
        for s in range(nh):
            rows = pl.ds(pl.multiple_of(j * tq + s * tk, tk), tk)
            kc = jnp.concatenate([k_ref[rows, hh * LANES:(hh + 1) * LANES],
                                  e_ref[s * tk:(s + 1) * tk, :]], axis=1)
            for c in range(2):
                z = _dot(kc, qaug_scr[2 * hh + c])
                z_scrs[hh][c * nh + s] = z
                zmax_scrs[hh][c * nh + s] = jnp.max(z, axis=0, keepdims=True)

    def softmax_pv(hh, j, diagonal):
        vt = jnp.concatenate([v_ref[(tq // LANES) * j + s, hh * LANES:(hh + 1) * LANES, :]
                              for s in range(tq // LANES)], axis=1)
        vt = jnp.concatenate([vt, jnp.ones((DIFF_ACC_ROWS - DIFF_V_DIM, tq), BF16)], axis=0)
        koff = (jnp.zeros((1, tq), jnp.int32) + j * tq).astype(F32) * slopes[hh]
        kr = lax.broadcasted_iota(jnp.int32, (tk, tq), 0)
        qc = lax.broadcasted_iota(jnp.int32, (tk, tq), 1)
        for c in range(2):
            zs = [z_scrs[hh][c * nh + s] for s in range(nh)]
            if diagonal:
                zs = [jnp.where(kr + s * tk <= qc, z, NEG_INF) for s, z in enumerate(zs)]
                zmaxs = [jnp.max(z, axis=0, keepdims=True) for z in zs]
            else:
                zmaxs = [zmax_scrs[hh][c * nh + s] for s in range(nh)]
            m_old = m_scr[2 * hh + c]
            m_tile = functools.reduce(jnp.maximum, zmaxs)
            m_new = jnp.maximum(m_old, m_tile + koff)
            alpha = jnp.exp2(m_old - m_new)
            shift = m_new - koff
            p = jnp.concatenate([jnp.exp2(z - shift).astype(BF16) for z in zs], axis=0)
            acc_scr[2 * hh + c] = alpha * acc_scr[2 * hh + c] + _dot(vt, p)
            m_scr[2 * hh + c] = m_new

    group_a = range(0, DIFF_HPS // 2)
    group_b = range(DIFF_HPS // 2, DIFF_HPS)
    for hh in group_a:
        build_queries(hh)
    for hh in group_a:
        qk(hh, 0)
    for hh in group_b:
        build_queries(hh)
    m_scr[...] = jnp.full_like(m_scr, NEG_INF)
    acc_scr[...] = jnp.zeros_like(acc_scr)

    def body(j, carry):
        for hh in group_b:
            qk(hh, j)
        for hh in group_a:
            softmax_pv(hh, j, False)
        for hh in group_a:
            qk(hh, j + 1)
        for hh in group_b:
            softmax_pv(hh, j, False)
        return carry

    lax.fori_loop(0, i, body, 0)
    for hh in group_b:
        qk(hh, i)
    for hh in range(DIFF_HPS):
        softmax_pv(hh, i, True)

    lp = lam_ref[...]
    lam = (jnp.exp(jnp.sum(lp[0:1, :] * lp[1:2, :], axis=1, keepdims=True))
           - jnp.exp(jnp.sum(lp[2:3, :] * lp[3:4, :], axis=1, keepdims=True)) + lam_init)
    for hh in range(DIFF_HPS):
        a0, a1 = acc_scr[2 * hh], acc_scr[2 * hh + 1]
        o0 = a0[0:DIFF_V_DIM, :] * (1.0 / a0[DIFF_V_DIM:DIFF_V_DIM + 1, :])
        o1 = a1[0:DIFF_V_DIM, :] * (1.0 / a1[DIFF_V_DIM:DIFF_V_DIM + 1, :])
        o = o0 - lam * o1
        y = o * lax.rsqrt(jnp.mean(o * o, axis=0, keepdims=True) + NORM_EPS)
        o_ref[:, hh * DIFF_V_DIM:(hh + 1) * DIFF_V_DIM] = ((y.T * g_ref[...]) * (1.0 - lam_init)).astype(BF16)


def _diff_attn(slopes, lam_params, subln, proj, proj_t, *, lam_init):
    tq, tk, hps = DIFF_TQ, DIFF_TK, DIFF_HPS
    w = hps * LANES
    k_col0 = N_GATE // w
    v_row0 = DIFF_WIDTH // w
    r = np.arange(tq)[:, None]
    c = np.arange(LANES)[None, :]
    e = jnp.asarray(np.where(c < 2, r % tk, np.where(c < 4, r // tk, 0)), dtype=BF16)
    return pl.pallas_call(
        functools.partial(_diff_body, lam_init=lam_init),
        grid=(DIFF_HEADS // hps, SEQ // tq),
        in_specs=[
            pl.BlockSpec(memory_space=pltpu.SMEM),
            pl.BlockSpec((4, DIFF_HEAD_DIM), lambda h, i: (0, 0)),
            pl.BlockSpec((tq // LANES, w, LANES), lambda h, i: (i, h, 0)),
            pl.BlockSpec((SEQ, w), lambda h, i: (0, k_col0 + h)),
            pl.BlockSpec((N_SLABS, w, LANES), lambda h, i: (0, v_row0 + h, 0)),
            pl.BlockSpec((tq, LANES), lambda h, i: (0, 0)),
            pl.BlockSpec((1, DIFF_V_DIM), lambda h, i: (0, 0)),
        ],
        out_specs=pl.BlockSpec((tq, hps * DIFF_V_DIM), lambda h, i: (i, h)),
        out_shape=jax.ShapeDtypeStruct((SEQ, DIFF_WIDTH), BF16),
        scratch_shapes=[
            pltpu.VMEM((2 * hps, 2 * LANES, tq), BF16),
            *[pltpu.VMEM((2 * (tq // tk), tk, tq), F32)] * hps,
            *[pltpu.VMEM((2 * (tq // tk), 1, tq), F32)] * hps,
            pltpu.VMEM((2 * hps, 1, tq), F32),
            pltpu.VMEM((2 * hps, DIFF_ACC_ROWS, tq), F32),
        ],
        compiler_params=_params("arbitrary", "arbitrary"),
        name="diff_attn",
    )(slopes, lam_params, proj_t, proj, proj_t, e, subln)


SWA_TQ = 512
SWA_BLOCKS = SWA_TQ // WINDOW


def _swa_bias(slope, off):
    kb = lax.broadcasted_iota(jnp.int32, (2 * WINDOW, WINDOW), 0)
    qa = lax.broadcasted_iota(jnp.int32, (2 * WINDOW, WINDOW), 1)
    dist = off + qa - kb
    return jnp.where((dist >= 0) & (dist < WINDOW), -slope * dist.astype(F32), NEG_INF)


def _swa_body(slopes_ref, sinks_ref, q_ref, k_ref, v_ref, o_ref, qaug_scr, bias_scr):
    i = pl.program_id(0)

    @pl.when(i == 0)
    def _():
        qaug_scr[...] = jnp.zeros_like(qaug_scr)
        for hd in range(SWA_Q_HEADS):
            bias_scr[hd] = _swa_bias(slopes_ref[hd], WINDOW)

    def block(t, slab0, bias_of):
        kwin = k_ref[pl.ds(pl.multiple_of(slab0 * WINDOW, WINDOW), 2 * WINDOW), :]
        vt = jnp.concatenate([v_ref[slab0], v_ref[slab0 + 1]], axis=1)
        ones = jnp.ones((16, 2 * WINDOW), BF16)
        qs = q_ref[t]
        zs = []
        for kh in range(SWA_KV_HEADS):
            r0 = kh * SWA_HEAD_DIM
            for g in range(SWA_GROUP):
                hd = kh * SWA_GROUP + g
                qh = qs[hd * SWA_HEAD_DIM:(hd + 1) * SWA_HEAD_DIM, :].astype(F32) * (SWA_HEAD_DIM ** -0.5 * LOG2E)
                qaug_scr[kh, r0:r0 + SWA_HEAD_DIM, g * LANES:(g + 1) * LANES] = qh.astype(BF16)
            zs.append(_dot(kwin, qaug_scr[kh]))
        outs = []
        for kh in range(SWA_KV_HEADS):
            r0 = kh * SWA_HEAD_DIM
            z = zs[kh]
            vt_kh = jnp.concatenate([vt[r0:r0 + SWA_HEAD_DIM, :], ones], axis=0)
            row = []
            for g in range(SWA_GROUP):
                hd = kh * SWA_GROUP + g
                sink = sinks_ref[hd]
                sc = z[:, g * LANES:(g + 1) * LANES] + bias_of(hd)
                m = jnp.maximum(jnp.max(sc, axis=0, keepdims=True), sink)
                p = jnp.exp2(sc - m).astype(BF16)
                ot = _dot(vt_kh, p)
                denom = ot[SWA_HEAD_DIM:SWA_HEAD_DIM + 1, :] + jnp.exp2(sink - m)
                row.append(ot[0:SWA_HEAD_DIM, :] * (1.0 / denom))
            outs.append(row)
        r = pl.multiple_of(t * WINDOW, WINDOW)
        for pr in range(SWA_KV_HEADS // 2):
            for g in range(SWA_GROUP):
                x = jnp.concatenate([outs[2 * pr][g], outs[2 * pr + 1][g]], axis=0)
                c0 = (pr * SWA_GROUP + g) * LANES
                o_ref[pl.ds(r, WINDOW), c0:c0 + LANES] = x.T.astype(BF16)

    @pl.when(i == 0)
    def _():
        block(0, 0, lambda hd: _swa_bias(slopes_ref[hd], 0))

    def body(t, carry):
        block(t, i * SWA_BLOCKS + t - 1, lambda hd: bias_scr[hd])
        return carry

    lax.fori_loop(jnp.where(i == 0, 1, 0), SWA_BLOCKS, body, 0)


def _swa_attn(slopes, sinks, proj, proj_t):
    q_row_blk = (2 * DIFF_WIDTH) // SWA_WIDTH
    k_col_blk = (N_GATE + DIFF_WIDTH) // SWA_KV_WIDTH
    v_row_blk = (2 * DIFF_WIDTH + SWA_WIDTH) // SWA_KV_WIDTH
    return pl.pallas_call(
        _swa_body,
        grid=(SEQ // SWA_TQ,),
        in_specs=[
            pl.BlockSpec(memory_space=pltpu.SMEM),
            pl.BlockSpec(memory_space=pltpu.SMEM),
            pl.BlockSpec((SWA_BLOCKS, SWA_WIDTH, LANES), lambda i: (i, q_row_blk, 0)),
            pl.BlockSpec((SEQ, SWA_KV_WIDTH), lambda i: (0, k_col_blk)),
            pl.BlockSpec((N_SLABS, SWA_KV_WIDTH, LANES), lambda i: (0, v_row_blk, 0)),
        ],
        out_specs=pl.BlockSpec((SWA_TQ, SWA_WIDTH), lambda i: (i, 0)),
        out_shape=jax.ShapeDtypeStruct((SEQ, SWA_WIDTH), BF16),
        scratch_shapes=[pltpu.VMEM((SWA_KV_HEADS, SWA_KV_WIDTH, SWA_GROUP * LANES), BF16),
                        pltpu.VMEM((SWA_Q_HEADS, 2 * WINDOW, WINDOW), F32)],
        compiler_params=_params("arbitrary"),
        name="swa_attn",
    )(slopes, sinks, proj_t, proj, proj_t)


def _mix_body(a_ref, b_ref, ga_ref, gb_ref, wa_ref, wb_ref, wo_ref, x_ref, gpost_ref, o_ref):
    j = pl.program_id(1)
    last = pl.num_programs(1) - 1
    tm = o_ref.shape[0]

    def step(chunk, first, final):
        for r in range(0, tm, chunk):
            rows = slice(r, r + chunk)
            ya = _dot(a_ref[rows, :], wa_ref[...])
            yb = _dot(b_ref[rows, :], wb_ref[...])
            mixed = ga_ref[rows, :].astype(F32) * ya + gb_ref[rows, :].astype(F32) * yb
            y = _dot(mixed.astype(BF16), wo_ref[...])
            if not first:
                y = o_ref[rows, :] + y
            if final:
                y = x_ref[rows, :] + _rms(y) * gpost_ref[...]
            o_ref[rows, :] = y

    pl.when(j == 0)(lambda: step(tm, True, False))
    pl.when((j > 0) & (j < last))(lambda: step(tm, False, False))
    pl.when(j == last)(lambda: step(FFN_EDGE_ROWS, False, True))


def _mix_out(a, b, proj, wa, wb, wo, x, gpost, *, tm=512, tn=1024):
    s, d = x.shape
    nj = d // tn
    return pl.pallas_call(
        _mix_body,
        grid=(s // tm, nj),
        in_specs=[
            pl.BlockSpec((tm, DIFF_WIDTH), lambda i, j: (i, 0)),
            pl.BlockSpec((tm, SWA_WIDTH), lambda i, j: (i, 0)),
            pl.BlockSpec((tm, tn), lambda i, j: (i, j)),
            pl.BlockSpec((tm, tn), lambda i, j: (i, nj + j)),
            pl.BlockSpec((DIFF_WIDTH, tn), lambda i, j: (0, j)),
            pl.BlockSpec((SWA_WIDTH, tn), lambda i, j: (0, j)),
            pl.BlockSpec((tn, d), lambda i, j: (j, 0)),
            pl.BlockSpec((tm, d), lambda i, j: (i, 0)),
            pl.BlockSpec((1, d), lambda i, j: (0, 0)),
        ],
        out_specs=pl.BlockSpec((tm, d), lambda i, j: (i, 0)),
        out_shape=jax.ShapeDtypeStruct((s, d), F32),
        compiler_params=_params("arbitrary", "arbitrary"),
        name="mix_out",
    )(a, b, proj, proj, wa, wb, wo, x, gpost)


def _alibi_slopes(n):
    return jnp.asarray(2.0 ** (-8.0 * np.arange(1, n + 1) / n), dtype=F32)


def kernel(x, ffn1_norm_pre, ffn1_w_in, ffn1_w_out, ffn1_norm_post, mix_norm_pre, w_in, diff_lambda, diff_subln, swa_sinks, w_branch_diff, w_branch_swa, w_out, mix_norm_post, ffn2_norm_pre, ffn2_w_in, ffn2_w_out, ffn2_norm_post):
    depth = ffn1_w_in.shape[0]
    xs = x.reshape(SEQ, D_MODEL)
    for l in range(depth):
        lam_init = 0.8 - 0.6 * math.exp(-0.3 * l)
        w = w_in[l]
        wbs = (w_branch_swa[l].reshape(2, 2, SWA_GROUP, SWA_HEAD_DIM, D_MODEL)
               .transpose(0, 2, 1, 3, 4).reshape(SWA_WIDTH, D_MODEL).astype(BF16))
        row = lambda v: v.reshape(1, -1)

        x1, h2 = _ffn(xs, row(ffn1_norm_pre[l]), ffn1_w_in[l].astype(BF16), ffn1_w_out[l].astype(BF16),
                      row(ffn1_norm_post[l]), row(mix_norm_pre[l]))
        proj, w2_in, w2_out = _proj(h2, w, convert=(ffn2_w_in[l], ffn2_w_out[l]))
        proj_t, wbd, wo = _proj_t(h2, w, convert=(w_branch_diff[l], w_out[l]))
        a = _diff_attn(_alibi_slopes(DIFF_HEADS) * LOG2E, diff_lambda[l], row(diff_subln[l]), proj, proj_t,
                       lam_init=lam_init)
        b = _swa_attn(_alibi_slopes(SWA_Q_HEADS) * LOG2E, swa_sinks[l] * LOG2E, proj, proj_t)
        x2 = _mix_out(a, b, proj, wbd, wbs, wo, x1, row(mix_norm_post[l]))
        (xs,) = _ffn(x2, row(ffn2_norm_pre[l]), w2_in, w2_out, row(ffn2_norm_post[l]))
    return xs.reshape(x.shape)
```

```python
import functools
import math

import numpy as np
import jax
import jax.numpy as jnp
from jax import lax
from jax.experimental import pallas as pl
from jax.experimental.pallas import tpu as pltpu

D_MODEL = 2048
SEQ = 8192
DIFF_HEADS = 8
DIFF_HEAD_DIM = 64
DIFF_V_DIM = 128
DIFF_WIDTH = 1024
SWA_Q_HEADS = 16
SWA_KV_HEADS = 4
SWA_GROUP = 4
SWA_HEAD_DIM = 64
SWA_WIDTH = 1024
SWA_KV_WIDTH = 256
WINDOW = 128
D_FF = 5504
NORM_EPS = 1e-6
NEG_INF = -1e30

LANES = 128
MXU_DIM = 256
VMEM_LIMIT_BYTES = 60 * 1024 * 1024

N_GATE = 2 * D_MODEL
PROJ_COLS = N_GATE + DIFF_WIDTH + SWA_KV_WIDTH
PROJT_ROWS = DIFF_WIDTH + DIFF_WIDTH + SWA_WIDTH + SWA_KV_WIDTH
N_SLABS = SEQ // LANES

BF16 = jnp.bfloat16
F32 = jnp.float32


def _dot(a, b):
    return jnp.dot(a, b, preferred_element_type=F32)


def _rms(x):
    return x * lax.rsqrt(jnp.mean(x * x, axis=-1, keepdims=True) + NORM_EPS)


def _params(*sem, flags=None):
    return pltpu.CompilerParams(dimension_semantics=sem, vmem_limit_bytes=VMEM_LIMIT_BYTES, flags=flags)


def _rider_specs(mats, n_steps, step_of):
    specs = []
    for m in mats:
        rb = 16 * pl.cdiv(m.shape[0], 16 * n_steps)
        while m.shape[0] % rb:
            rb += 16
        specs.append(pl.BlockSpec((rb, m.shape[1]),
                                  lambda *ids, nb=m.shape[0] // rb: (jnp.minimum(step_of(*ids), nb - 1), 0)))
    return specs


def _round_riders(srcs, dsts):
    for src, dst in zip(srcs, dsts):
        dst[...] = src[...].astype(BF16)


FFN_ROW_CHUNK = 512
FFN_EDGE_ROWS = 256


def _ffn_body(*refs, emit_next, overlap, n_convert):
    refs = list(refs)
    x_ref, gpre_ref, wg_ref, wu_ref, wo_ref, gpost_ref = refs[:6]
    del refs[:6]
    gnext_ref = refs.pop(0) if emit_next else None
    cvt_in = [refs.pop(0) for _ in range(n_convert)]
    o_ref = refs.pop(0)
    hn_ref = refs.pop(0) if emit_next else None
    cvt_out = [refs.pop(0) for _ in range(n_convert)]
    (h_scr,) = refs
    j = pl.program_id(1)
    last = pl.num_programs(1) - 1
    tm = h_scr.shape[0]

    def step(chunk, first, final):
        _round_riders(cvt_in, cvt_out)
        for r in range(0, tm, chunk):
            rows = slice(r, r + chunk)
            if first:
                h = (_rms(x_ref[rows, :]) * gpre_ref[...]).astype(BF16)
                h_scr[rows, :] = h
            else:
                h = h_scr[rows, :]
            g = _dot(h, wg_ref[...])
            u = _dot(h, wu_ref[...])
            hg = 0.5 * g
            act = (hg + hg * jnp.tanh(hg)) * u
            if final and overlap:
                col = lax.broadcasted_iota(jnp.int32, act.shape, 1)
                act = jnp.where(col < overlap, 0.0, act)
            y = _dot(act.astype(BF16), wo_ref[...])
            if not first:
                y = o_ref[rows, :] + y
            if final:
                y = x_ref[rows, :] + 0.5 * (_rms(y) * gpost_ref[...])
                if emit_next:
                    hn_ref[rows, :] = (_rms(y) * gnext_ref[...]).astype(BF16)
            o_ref[rows, :] = y

    pl.when(j == 0)(lambda: step(FFN_EDGE_ROWS, True, False))
    pl.when((j > 0) & (j < last))(lambda: step(FFN_ROW_CHUNK, False, False))
    pl.when(j == last)(lambda: step(FFN_EDGE_ROWS, False, True))


def _ffn(x, gpre, w_in, w_out, gpost, gnext=None, convert=(), *, tm=1024, tf=512):
    emit_next = gnext is not None
    s, d = x.shape
    d_ff = w_out.shape[0]
    nj = pl.cdiv(d_ff, tf)
    overlap = nj * tf - d_ff
    assert tf % LANES == 0 and d_ff % LANES == 0
    start = lambda j, base=0: (jnp.minimum(j * (tf // LANES), (d_ff - tf) // LANES) + base // LANES) * LANES
    row = lambda i, j: (i, 0)
    fixed = lambda i, j: (0, 0)
    in_specs = [
        pl.BlockSpec((tm, d), row),
        pl.BlockSpec((1, d), fixed),
        pl.BlockSpec((pl.Element(d), pl.Element(tf)), lambda i, j: (0, start(j))),
        pl.BlockSpec((pl.Element(d), pl.Element(tf)), lambda i, j: (0, start(j, d_ff))),
        pl.BlockSpec((pl.Element(tf), pl.Element(d)), lambda i, j: (start(j), 0)),
        pl.BlockSpec((1, d), fixed),
    ]
    args = [x, gpre, w_in, w_in, w_out, gpost]
    out_shape = [jax.ShapeDtypeStruct((s, d), F32)]
    out_specs = [pl.BlockSpec((tm, d), row)]
    if emit_next:
        in_specs.append(pl.BlockSpec((1, d), fixed))
        args.append(gnext)
        out_shape.append(jax.ShapeDtypeStruct((s, d), BF16))
        out_specs.append(pl.BlockSpec((tm, d), row, pipeline_mode=pl.Buffered(1)))
    riders = _rider_specs(convert, (s // tm) * nj, lambda i, j: i * nj + j)
    in_specs += riders
    args += list(convert)
    out_specs += riders
    out_shape += [jax.ShapeDtypeStruct(m.shape, BF16) for m in convert]
    res = pl.pallas_call(
        functools.partial(_ffn_body, emit_next=emit_next, overlap=overlap, n_convert=len(convert)),
        grid=(s // tm, nj),
        in_specs=in_specs,
        out_specs=out_specs,
        out_shape=out_shape,
        scratch_shapes=[pltpu.VMEM((tm, d), BF16)],
        compiler_params=_params("arbitrary", "arbitrary"),
        name="ffn_next" if emit_next else "ffn",
    )(*args)
    return res


PROJ_TN = 2 * MXU_DIM
assert 2 * SWA_KV_WIDTH == PROJ_TN
_O_DQ, _O_DK, _O_DV, _O_SQ = (k * DIFF_WIDTH // PROJ_TN for k in range(4))
_O_SKV = _O_SQ + SWA_WIDTH // PROJ_TN
_O_G = _O_SKV + 1
_N_G, _N_D, _N_S = N_GATE // PROJ_TN, DIFF_WIDTH // PROJ_TN, SWA_WIDTH // PROJ_TN


def _proj_src_tile(t):
    return jnp.where(t < _N_G, _O_G + t, jnp.where(t < _N_G + _N_D, _O_DK + (t - _N_G), _O_SKV))


def _proj_t_src_tile(t):
    return jnp.where(t < _N_D, _O_DQ + t,
                     jnp.where(t < 2 * _N_D, _O_DV + (t - _N_D),
                               jnp.where(t < 2 * _N_D + _N_S, _O_SQ + (t - 2 * _N_D), _O_SKV)))


PROJ_ROW_CHUNK = 256


def _proj_body(h_ref, w_ref, *refs):
    n = (len(refs) - 2) // 2
    cvt_in, o_ref, cvt_out, w_scr = refs[:n], refs[n], refs[n + 1:2 * n + 1], refs[2 * n + 1]

    def run(gated, width):
        w_scr[...] = w_ref[...].astype(BF16)
        _round_riders(cvt_in, cvt_out)
        for r in range(0, h_ref.shape[0], PROJ_ROW_CHUNK):
            rows = slice(r, r + PROJ_ROW_CHUNK)
            res = _dot(h_ref[rows, :], w_scr[:, 0:width])
            if gated:
                res = 0.5 * jnp.tanh(0.5 * res) + 0.5
            o_ref[rows, 0:width] = res.astype(BF16)

    t = pl.program_id(1)
    last = pl.num_programs(1) - 1
    pl.when(t < _N_G)(lambda: run(True, PROJ_TN))
    pl.when((t >= _N_G) & (t < last))(lambda: run(False, PROJ_TN))
    pl.when(t == last)(lambda: run(False, SWA_KV_WIDTH))


def _proj(h, w, convert=(), *, tm=2048):
    s, d = h.shape
    tn = PROJ_TN
    nt = pl.cdiv(PROJ_COLS, tn)
    riders = _rider_specs(convert, (s // tm) * nt, lambda i, t: i * nt + t)
    return pl.pallas_call(
        _proj_body,
        grid=(s // tm, nt),
        in_specs=[pl.BlockSpec((tm, d), lambda i, t: (i, 0)),
                  pl.BlockSpec((d, tn), lambda i, t: (0, _proj_src_tile(t)))] + riders,
        out_specs=[pl.BlockSpec((tm, tn), lambda i, t: (i, t))] + riders,
        out_shape=[jax.ShapeDtypeStruct((s, PROJ_COLS), BF16)]
                  + [jax.ShapeDtypeStruct(m.shape, BF16) for m in convert],
        scratch_shapes=[pltpu.VMEM((d, tn), BF16)],
        compiler_params=_params("arbitrary", "arbitrary"),
        name="proj",
    )(h, w, *convert)


def _proj_t_body(h_ref, w_ref, *refs):
    n = (len(refs) - 2) // 2
    cvt_in, o_ref, cvt_out, w_scr = refs[:n], refs[n], refs[n + 1:2 * n + 1], refs[2 * n + 1]
    def run(c0, width):
        w_scr[...] = w_ref[...].astype(BF16)
        _round_riders(cvt_in, cvt_out)
        for r in range(0, h_ref.shape[0], PROJ_ROW_CHUNK):
            res = _dot(h_ref[r:r + PROJ_ROW_CHUNK, :], w_scr[:, c0:c0 + width])
            for k in range(PROJ_ROW_CHUNK // LANES):
                o_ref[r // LANES + k, 0:width, :] = res[k * LANES:(k + 1) * LANES, :].T.astype(BF16)

    t = pl.program_id(1)
    last = pl.num_programs(1) - 1
    pl.when(t < last)(lambda: run(0, PROJ_TN))
    pl.when(t == last)(lambda: run(SWA_KV_WIDTH, SWA_KV_WIDTH))


def _proj_t(h, w, convert=(), *, tm=2048):
    s, d = h.shape
    tn = PROJ_TN
    nt = pl.cdiv(PROJT_ROWS, tn)
    riders = _rider_specs(convert, (s // tm) * nt, lambda i, t: i * nt + t)
    return pl.pallas_call(
        _proj_t_body,
        grid=(s // tm, nt),
        in_specs=[pl.BlockSpec((tm, d), lambda i, t: (i, 0)),
                  pl.BlockSpec((d, tn), lambda i, t: (0, _proj_t_src_tile(t)))] + riders,
        out_specs=[pl.BlockSpec((tm // LANES, tn, LANES), lambda i, t: (i, t, 0))] + riders,
        out_shape=[jax.ShapeDtypeStruct((s // LANES, PROJT_ROWS, LANES), BF16)]
                  + [jax.ShapeDtypeStruct(m.shape, BF16) for m in convert],
        scratch_shapes=[pltpu.VMEM((d, tn), BF16)],
        compiler_params=_params("arbitrary", "arbitrary"),
        name="proj_t",
    )(h, w, *convert)


DIFF_TQ = 512
DIFF_TK = 256
DIFF_ACC_ROWS = DIFF_V_DIM + 16
DIFF_HPS = 4
LOG2E = math.log2(math.e)


def _diff_body(slopes_ref, lam_ref, q_ref, k_ref, v_ref, e_ref, g_ref, o_ref, qaug_scr, *scr, lam_init):
    tq, tk, nh = DIFF_TQ, DIFF_TK, DIFF_TQ // DIFF_TK
    hp = pl.program_id(0)
    i = pl.program_id(1)
    z_scrs, zmax_scrs = scr[:DIFF_HPS], scr[DIFF_HPS:2 * DIFF_HPS]
    m_scr, acc_scr = scr[2 * DIFF_HPS:]
    slopes = [slopes_ref[hp * DIFF_HPS + hh] for hh in range(DIFF_HPS)]

    row = lax.broadcasted_iota(jnp.int32, (LANES, tq), 0)

    def build_queries(hh):
        q = jnp.concatenate([q_ref[s, hh * LANES:(hh + 1) * LANES, :] for s in range(tq // LANES)], axis=1)
        qs = (q.astype(F32) * (DIFF_HEAD_DIM ** -0.5 * LOG2E)).astype(BF16)
        zero = jnp.zeros_like(qs)
        s_f32 = jnp.where(row < 2, slopes[hh], jnp.where(row < 4, slopes[hh] * tk, 0.0))
        s_hi = s_f32.astype(BF16)
        s_lo = (s_f32 - s_hi.astype(F32)).astype(BF16)
        srow = jnp.where(row % 2 == 0, s_hi, s_lo)
        qaug_scr[2 * hh, 0:LANES, :] = jnp.where(row < DIFF_HEAD_DIM, qs, zero)
        qaug_scr[2 * hh + 1, 0:LANES, :] = jnp.where(row >= DIFF_HEAD_DIM, qs, zero)
        qaug_scr[2 * hh, LANES:2 * LANES, :] = srow
        qaug_scr[2 * hh + 1, LANES:2 * LANES, :] = srow

    def qk(hh, j):
        for s in range(nh):
            rows = pl.ds(pl.multiple_of(j * tq + s * tk, tk), tk)
            kc = jnp.concatenate([k_ref[rows, hh * LANES:(hh + 1) * LANES],
                                  e_ref[s * tk:(s + 1) * tk, :]], axis=1)
            for c in range(2):
                z = _dot(kc, qaug_scr[2 * hh + c])
                z_scrs[hh][c * nh + s] = z
                zmax_scrs[hh][c * nh + s] = jnp.max(z, axis=0, keepdims=True)

    def softmax_pv(hh, j, diagonal):
        vt = jnp.concatenate([v_ref[(tq // LANES) * j + s, hh * LANES:(hh + 1) * LANES, :]
                              for s in range(tq // LANES)], axis=1)
        vt = jnp.concatenate([vt, jnp.ones((DIFF_ACC_ROWS - DIFF_V_DIM, tq), BF16)], axis=0)
        koff = (jnp.zeros((1, tq), jnp.int32) + j * tq).astype(F32) * slopes[hh]
        kr = lax.broadcasted_iota(jnp.int32, (tk, tq), 0)
        qc = lax.broadcasted_iota(jnp.int32, (tk, tq), 1)
        for c in range(2):
            zs = [z_scrs[hh][c * nh + s] for s in range(nh)]
            if diagonal:
                zs = [jnp.where(kr + s * tk <= qc, z, NEG_INF) for s, z in enumerate(zs)]
                zmaxs = [jnp.max(z, axis=0, keepdims=True) for z in zs]
            else:
                zmaxs = [zmax_scrs[hh][c * nh + s] for s in range(nh)]
            m_old = m_scr[2 * hh + c]
            m_tile = functools.reduce(jnp.maximum, zmaxs)
            m_new = jnp.maximum(m_old, m_tile + koff)
            alpha = jnp.exp2(m_old - m_new)
            shift = m_new - koff
            p = jnp.concatenate([jnp.exp2(z - shift).astype(BF16) for z in zs], axis=0)
            acc_scr[2 * hh + c] = alpha * acc_scr[2 * hh + c] + _dot(vt, p)
            m_scr[2 * hh + c] = m_new

    group_a = range(0, DIFF_HPS // 2)
    group_b = range(DIFF_HPS // 2, DIFF_HPS)
    for hh in group_a:
        build_queries(hh)
    for hh in group_a:
        qk(hh, 0)
    for hh in group_b:
        build_queries(hh)
    m_scr[...] = jnp.full_like(m_scr, NEG_INF)
    acc_scr[...] = jnp.zeros_like(acc_scr)

    def body(j, carry):
        for hh in group_b:
            qk(hh, j)
        for hh in group_a:
            softmax_pv(hh, j, False)
        for hh in group_a:
            qk(hh, j + 1)
        for hh in group_b:
            softmax_pv(hh, j, False)
        return carry

    lax.fori_loop(0, i, body, 0)
    for hh in group_b:
        qk(hh, i)
    for hh in range(DIFF_HPS):
        softmax_pv(hh, i, True)

    lp = lam_ref[...]
    lam = (jnp.exp(jnp.sum(lp[0:1, :] * lp[1:2, :], axis=1, keepdims=True))
           - jnp.exp(jnp.sum(lp[2:3, :] * lp[3:4, :], axis=1, keepdims=True)) + lam_init)
    for hh in range(DIFF_HPS):
        a0, a1 = acc_scr[2 * hh], acc_scr[2 * hh + 1]
        o0 = a0[0:DIFF_V_DIM, :] * (1.0 / a0[DIFF_V_DIM:DIFF_V_DIM + 1, :])
        o1 = a1[0:DIFF_V_DIM, :] * (1.0 / a1[DIFF_V_DIM:DIFF_V_DIM + 1, :])
        o = o0 - lam * o1
        y = o * lax.rsqrt(jnp.mean(o * o, axis=0, keepdims=True) + NORM_EPS)
        o_ref[:, hh * DIFF_V_DIM:(hh + 1) * DIFF_V_DIM] = ((y.T * g_ref[...]) * (1.0 - lam_init)).astype(BF16)


def _diff_attn(slopes, lam_params, subln, proj, proj_t, *, lam_init):
    tq, tk, hps = DIFF_TQ, DIFF_TK, DIFF_HPS
    w = hps * LANES
    k_col0 = N_GATE // w
    v_row0 = DIFF_WIDTH // w
    r = np.arange(tq)[:, None]
    c = np.arange(LANES)[None, :]
    e = jnp.asarray(np.where(c < 2, r % tk, np.where(c < 4, r // tk, 0)), dtype=BF16)
    return pl.pallas_call(
        functools.partial(_diff_body, lam_init=lam_init),
        grid=(DIFF_HEADS // hps, SEQ // tq),
        in_specs=[
            pl.BlockSpec(memory_space=pltpu.SMEM),
            pl.BlockSpec((4, DIFF_HEAD_DIM), lambda h, i: (0, 0)),
            pl.BlockSpec((tq // LANES, w, LANES), lambda h, i: (i, h, 0)),
            pl.BlockSpec((SEQ, w), lambda h, i: (0, k_col0 + h)),
            pl.BlockSpec((N_SLABS, w, LANES), lambda h, i: (0, v_row0 + h, 0)),
            pl.BlockSpec((tq, LANES), lambda h, i: (0, 0)),
            pl.BlockSpec((1, DIFF_V_DIM), lambda h, i: (0, 0)),
        ],
        out_specs=pl.BlockSpec((tq, hps * DIFF_V_DIM), lambda h, i: (i, h)),
        out_shape=jax.ShapeDtypeStruct((SEQ, DIFF_WIDTH), BF16),
        scratch_shapes=[
            pltpu.VMEM((2 * hps, 2 * LANES, tq), BF16),
            *[pltpu.VMEM((2 * (tq // tk), tk, tq), F32)] * hps,
            *[pltpu.VMEM((2 * (tq // tk), 1, tq), F32)] * hps,
            pltpu.VMEM((2 * hps, 1, tq), F32),
            pltpu.VMEM((2 * hps, DIFF_ACC_ROWS, tq), F32),
        ],
        compiler_params=_params("arbitrary", "arbitrary"),
        name="diff_attn",
    )(slopes, lam_params, proj_t, proj, proj_t, e, subln)


SWA_TQ = 512
SWA_BLOCKS = SWA_TQ // WINDOW


def _swa_bias(slope, off):
    kb = lax.broadcasted_iota(jnp.int32, (2 * WINDOW, WINDOW), 0)
    qa = lax.broadcasted_iota(jnp.int32, (2 * WINDOW, WINDOW), 1)
    dist = off + qa - kb
    return jnp.where((dist >= 0) & (dist < WINDOW), -slope * dist.astype(F32), NEG_INF)


def _swa_body(slopes_ref, sinks_ref, q_ref, k_ref, v_ref, o_ref, qaug_scr, bias_scr):
    i = pl.program_id(0)

    @pl.when(i == 0)
    def _():
        qaug_scr[...] = jnp.zeros_like(qaug_scr)
        for hd in range(SWA_Q_HEADS):
            bias_scr[hd] = _swa_bias(slopes_ref[hd], WINDOW)

    def block(t, slab0, bias_of):
        kwin = k_ref[pl.ds(pl.multiple_of(slab0 * WINDOW, WINDOW), 2 * WINDOW), :]
        vt = jnp.concatenate([v_ref[slab0], v_ref[slab0 + 1]], axis=1)
        ones = jnp.ones((16, 2 * WINDOW), BF16)
        qs = q_ref[t]
        zs = []
        for kh in range(SWA_KV_HEADS):
            r0 = kh * SWA_HEAD_DIM
            for g in range(SWA_GROUP):
                hd = kh * SWA_GROUP + g
                qh = qs[hd * SWA_HEAD_DIM:(hd + 1) * SWA_HEAD_DIM, :].astype(F32) * (SWA_HEAD_DIM ** -0.5 * LOG2E)
                qaug_scr[kh, r0:r0 + SWA_HEAD_DIM, g * LANES:(g + 1) * LANES] = qh.astype(BF16)
            zs.append(_dot(kwin, qaug_scr[kh]))
        outs = []
        for kh in range(SWA_KV_HEADS):
            r0 = kh * SWA_HEAD_DIM
            z = zs[kh]
            vt_kh = jnp.concatenate([vt[r0:r0 + SWA_HEAD_DIM, :], ones], axis=0)
            row = []
            for g in range(SWA_GROUP):
                hd = kh * SWA_GROUP + g
                sink = sinks_ref[hd]
                sc = z[:, g * LANES:(g + 1) * LANES] + bias_of(hd)
                m = jnp.maximum(jnp.max(sc, axis=0, keepdims=True), sink)
                p = jnp.exp2(sc - m).astype(BF16)
                ot = _dot(vt_kh, p)
                denom = ot[SWA_HEAD_DIM:SWA_HEAD_DIM + 1, :] + jnp.exp2(sink - m)
                row.append(ot[0:SWA_HEAD_DIM, :] * (1.0 / denom))
            outs.append(row)
        r = pl.multiple_of(t * WINDOW, WINDOW)
        for pr in range(SWA_KV_HEADS // 2):
            for g in range(SWA_GROUP):
                x = jnp.concatenate([outs[2 * pr][g], outs[2 * pr + 1][g]], axis=0)
                c0 = (pr * SWA_GROUP + g) * LANES
                o_ref[pl.ds(r, WINDOW), c0:c0 + LANES] = x.T.astype(BF16)

    @pl.when(i == 0)
    def _():
        block(0, 0, lambda hd: _swa_bias(slopes_ref[hd], 0))

    def body(t, carry):
        block(t, i * SWA_BLOCKS + t - 1, lambda hd: bias_scr[hd])
        return carry

    lax.fori_loop(jnp.where(i == 0, 1, 0), SWA_BLOCKS, body, 0)


def _swa_attn(slopes, sinks, proj, proj_t):
    q_row_blk = (2 * DIFF_WIDTH) // SWA_WIDTH
    k_col_blk = (N_GATE + DIFF_WIDTH) // SWA_KV_WIDTH
    v_row_blk = (2 * DIFF_WIDTH + SWA_WIDTH) // SWA_KV_WIDTH
    return pl.pallas_call(
        _swa_body,
        grid=(SEQ // SWA_TQ,),
        in_specs=[
            pl.BlockSpec(memory_space=pltpu.SMEM),
            pl.BlockSpec(memory_space=pltpu.SMEM),
            pl.BlockSpec((SWA_BLOCKS, SWA_WIDTH, LANES), lambda i: (i, q_row_blk, 0)),
            pl.BlockSpec((SEQ, SWA_KV_WIDTH), lambda i: (0, k_col_blk)),
            pl.BlockSpec((N_SLABS, SWA_KV_WIDTH, LANES), lambda i: (0, v_row_blk, 0)),
        ],
        out_specs=pl.BlockSpec((SWA_TQ, SWA_WIDTH), lambda i: (i, 0)),
        out_shape=jax.ShapeDtypeStruct((SEQ, SWA_WIDTH), BF16),
        scratch_shapes=[pltpu.VMEM((SWA_KV_HEADS, SWA_KV_WIDTH, SWA_GROUP * LANES), BF16),
                        pltpu.VMEM((SWA_Q_HEADS, 2 * WINDOW, WINDOW), F32)],
        compiler_params=_params("arbitrary"),
        name="swa_attn",
    )(slopes, sinks, proj_t, proj, proj_t)


def _mix_body(a_ref, b_ref, ga_ref, gb_ref, wa_ref, wb_ref, wo_ref, x_ref, gpost_ref, o_ref):
    for r in range(0, o_ref.shape[0], FFN_EDGE_ROWS):
        rows = slice(r, r + FFN_EDGE_ROWS)
        ya = _dot(a_ref[rows, :], wa_ref[...])
        yb = _dot(b_ref[rows, :], wb_ref[...])
        mixed = ga_ref[rows, :].astype(F32) * ya + gb_ref[rows, :].astype(F32) * yb
        y = _dot(mixed.astype(BF16), wo_ref[...])
        o_ref[rows, :] = x_ref[rows, :] + _rms(y) * gpost_ref[...]


def _mix_out(a, b, proj, wa, wb, wo, x, gpost, *, tm=512):
    s, d = x.shape
    resident = dict(pipeline_mode=pl.Buffered(1))
    return pl.pallas_call(
        _mix_body,
        grid=(s // tm,),
        in_specs=[
            pl.BlockSpec((tm, DIFF_WIDTH), lambda i: (i, 0)),
            pl.BlockSpec((tm, SWA_WIDTH), lambda i: (i, 0)),
            pl.BlockSpec((tm, d), lambda i: (i, 0)),
            pl.BlockSpec((tm, d), lambda i: (i, 1)),
            pl.BlockSpec((DIFF_WIDTH, d), lambda i: (0, 0), **resident),
            pl.BlockSpec((SWA_WIDTH, d), lambda i: (0, 0), **resident),
            pl.BlockSpec((d, d), lambda i: (0, 0), **resident),
            pl.BlockSpec((tm, d), lambda i: (i, 0)),
            pl.BlockSpec((1, d), lambda i: (0, 0)),
        ],
        out_specs=pl.BlockSpec((tm, d), lambda i: (i, 0)),
        out_shape=jax.ShapeDtypeStruct((s, d), F32),
        compiler_params=_params("arbitrary"),
        name="mix_out",
    )(a, b, proj, proj, wa, wb, wo, x, gpost)


def _alibi_slopes(n):
    return jnp.asarray(2.0 ** (-8.0 * np.arange(1, n + 1) / n), dtype=F32)


def kernel(x, ffn1_norm_pre, ffn1_w_in, ffn1_w_out, ffn1_norm_post, mix_norm_pre, w_in, diff_lambda, diff_subln, swa_sinks, w_branch_diff, w_branch_swa, w_out, mix_norm_post, ffn2_norm_pre, ffn2_w_in, ffn2_w_out, ffn2_norm_post):
    depth = ffn1_w_in.shape[0]
    xs = x.reshape(SEQ, D_MODEL)
    for l in range(depth):
        lam_init = 0.8 - 0.6 * math.exp(-0.3 * l)
        w = w_in[l]
        wbs = (w_branch_swa[l].reshape(2, 2, SWA_GROUP, SWA_HEAD_DIM, D_MODEL)
               .transpose(0, 2, 1, 3, 4).reshape(SWA_WIDTH, D_MODEL).astype(BF16))
        row = lambda v: v.reshape(1, -1)

        x1, h2 = _ffn(xs, row(ffn1_norm_pre[l]), ffn1_w_in[l].astype(BF16), ffn1_w_out[l].astype(BF16),
                      row(ffn1_norm_post[l]), row(mix_norm_pre[l]))
        proj, w2_in, w2_out = _proj(h2, w, convert=(ffn2_w_in[l], ffn2_w_out[l]))
        proj_t, wbd, wo = _proj_t(h2, w, convert=(w_branch_diff[l], w_out[l]))
        a = _diff_attn(_alibi_slopes(DIFF_HEADS) * LOG2E, diff_lambda[l], row(diff_subln[l]), proj, proj_t,
                       lam_init=lam_init)
        b = _swa_attn(_alibi_slopes(SWA_Q_HEADS) * LOG2E, swa_sinks[l] * LOG2E, proj, proj_t)
        x2 = _mix_out(a, b, proj, wbd, wbs, wo, x1, row(mix_norm_post[l]))
        (xs,) = _ffn(x2, row(ffn2_norm_pre[l]), w2_in, w2_out, row(ffn2_norm_post[l]))
    return xs.reshape(x.shape)
```
